```python
import jax, jax.numpy as jnp
from jax import lax
import numpy as np

D_MODEL = 1024
BATCH = 32
SEQ = 2048
DEPTH = 1
DEC_BATCH = 128
DEC_SEQ = 1
PAST_LEN = 16384
PAGE_SIZE = 128

HEAD_DIM = 128
RET_HEADS = 4
RET_W = RET_HEADS * HEAD_DIM
MLA_HEADS = 4
MLA_NOPE = 128
MLA_ROPE = 64
MLA_V = 128
MLA_W = MLA_HEADS * MLA_V
Q_LORA = 384
KV_LORA = 256
MEM_HEADS = 4
MEM_W = MEM_HEADS * HEAD_DIM
N_MEM = 256
D_MIX = RET_W + MLA_W + MEM_W
RET_CHUNK = 128
Q_BLOCK = 128
ROPE_BASE = 10000.0
EPS = 1e-6
MLA_SCALE = (MLA_NOPE + MLA_ROPE) ** -0.5
MEM_SCALE = HEAD_DIM ** -0.5
IN_SIZES = (RET_W, RET_W, RET_W, Q_LORA, KV_LORA, MLA_ROPE, MEM_W, D_MIX)
D_IN = RET_W * 3 + Q_LORA + KV_LORA + MLA_ROPE + MEM_W + D_MIX

kernel_name = 'hybrid_retention_mla_memory_step'


def _split_cols(z, sizes):
    out, o = [], 0
    for s in sizes:
        out.append(z[..., o:o + s])
        o += s
    return out


def rms_norm(x, g):
    xf = x.astype(jnp.float32)
    y = xf * lax.rsqrt(jnp.mean(xf * xf, axis=-1, keepdims=True) + EPS)
    return (y * g.astype(jnp.float32)).astype(x.dtype)


def rope(x, pos):
    d = x.shape[-1]
    inv = ROPE_BASE ** (-jnp.arange(0, d, 2, dtype=jnp.float32) / d)
    ang = pos.astype(jnp.float32)[:, None] * inv[None, :]
    if x.ndim == 4:
        ang = ang[:, None, :]
    c = jnp.cos(ang).astype(x.dtype)
    s = jnp.sin(ang).astype(x.dtype)
    x1, x2 = x[..., : d // 2], x[..., d // 2:]
    return jnp.concatenate([x1 * c - x2 * s, x2 * c + x1 * s], axis=-1)


def retention(q, k, v, state0):
    B, T, H, dk = q.shape
    dv = v.shape[-1]
    chunk = RET_CHUNK if T % RET_CHUNK == 0 else T
    n = T // chunk
    f32 = jnp.float32
    log_g = jnp.log1p(-jnp.exp2(-5.0 - jnp.arange(H, dtype=f32)))
    idx = jnp.arange(chunk, dtype=f32)
    diff = idx[:, None] - idx[None, :]
    intra = jnp.where(diff[None] >= 0, jnp.exp(jnp.maximum(diff, 0.0)[None] * log_g[:, None, None]), 0.0)
    q_dec = jnp.exp((idx[:, None] + 1.0) * log_g[None, :])
    k_dec = jnp.exp((chunk - 1.0 - idx)[:, None] * log_g[None, :])
    s_dec = jnp.exp(chunk * log_g)

    def to_chunks(a):
        return a.astype(f32).reshape(B, n, chunk, H, a.shape[-1]).transpose(1, 0, 2, 3, 4)

    def step(S, inp):
        qb, kb, vb = inp
        sc = jnp.einsum('bihd,bjhd->bhij', qb, kb) * intra[None]
        o = (jnp.einsum('bhij,bjhe->bihe', sc, vb)
             + jnp.einsum('bihd,bhde->bihe', qb * q_dec[None, :, :, None], S))
        S = S * s_dec[None, :, None, None] + jnp.einsum('bjhd,bjhe->bhde', kb * k_dec[None, :, :, None], vb)
        return S, o

    S, o = lax.scan(step, state0.astype(f32), (to_chunks(q), to_chunks(k), to_chunks(v)))
    o = o.transpose(1, 0, 2, 3, 4).reshape(B, T, H, dv)
    return o.astype(v.dtype), S.astype(state0.dtype)


def head_norm(o, g):
    B, T, H, dv = o.shape
    of = o.astype(jnp.float32)
    mu = jnp.mean(of, axis=-1, keepdims=True)
    var = jnp.mean((of - mu) ** 2, axis=-1, keepdims=True)
    y = ((of - mu) * lax.rsqrt(var + EPS)).reshape(B, T, H * dv)
    return (y * g.astype(jnp.float32)).astype(o.dtype)


def project(h, pos, norm_g, w_in, qnorm_g, w_uq, w_uk, kvnorm_g):
    B, T, _ = h.shape
    z = rms_norm(h, norm_g) @ w_in
    rq, rk, rv, cq, ckv, kpe, mq, gate = _split_cols(z, IN_SIZES)
    rq = rope(rq.reshape(B, T, RET_HEADS, HEAD_DIM), pos)
    rk = rope(rk.reshape(B, T, RET_HEADS, HEAD_DIM), pos) * (HEAD_DIM ** -0.5)
    rv = rv.reshape(B, T, RET_HEADS, HEAD_DIM)
    q = jnp.einsum('btc,chd->bthd', rms_norm(cq, qnorm_g), w_uq)
    q_nope = q[..., :MLA_NOPE]
    q_pe = rope(q[..., MLA_NOPE:], pos)
    q_lat = jnp.einsum('bthd,chd->bthc', q_nope, w_uk)
    ckv = rms_norm(ckv, kvnorm_g)
    kpe = rope(kpe, pos)
    mq = mq.reshape(B, T, MEM_HEADS, HEAD_DIM)
    return rq, rk, rv, q_lat, q_pe, ckv, kpe, mq, gate


def mla_prompt(q_lat, q_pe, ckv, kpe):
    B, S, H, C = q_lat.shape
    nb = S // Q_BLOCK
    qlb = q_lat.reshape(B, nb, Q_BLOCK, H, C).transpose(1, 0, 2, 3, 4)
    qpb = q_pe.reshape(B, nb, Q_BLOCK, H, MLA_ROPE).transpose(1, 0, 2, 3, 4)
    kpos = jnp.arange(S)

    def blk(args):
        ql, qp, i = args
        s = jnp.einsum('bqhc,bkc->bhqk', ql, ckv) + jnp.einsum('bqhr,bkr->bhqk', qp, kpe)
        qpos = i * Q_BLOCK + jnp.arange(Q_BLOCK)
        s = jnp.where(kpos[None, :] <= qpos[:, None], s.astype(jnp.float32) * MLA_SCALE, -jnp.inf)
        p = jax.nn.softmax(s, axis=-1).astype(ckv.dtype)
        return jnp.einsum('bhqk,bkc->bqhc', p, ckv)

    o = lax.map(blk, (qlb, qpb, jnp.arange(nb)))
    return o.transpose(1, 0, 2, 3, 4).reshape(B, S, H, C)


def mla_sample(q_lat, q_pe, ckv_new, kpe_new, pool_ckv, pool_kpe, page_table):
    DB, T, H, C = q_lat.shape
    ckv_past = pool_ckv[page_table].reshape(DB, -1, KV_LORA)
    kpe_past = pool_kpe[page_table].reshape(DB, -1, MLA_ROPE)
    P = ckv_past.shape[1]
    s_past = jnp.einsum('bqhc,bkc->bhqk', q_lat, ckv_past) + jnp.einsum('bqhr,bkr->bhqk', q_pe, kpe_past)
    s_new = jnp.einsum('bqhc,bkc->bhqk', q_lat, ckv_new) + jnp.einsum('bqhr,bkr->bhqk', q_pe, kpe_new)
    causal = jnp.tril(jnp.ones((T, T), dtype=bool))
    s_new = jnp.where(causal, s_new.astype(jnp.float32) * MLA_SCALE, -jnp.inf)
    s = jnp.concatenate([s_past.astype(jnp.float32) * MLA_SCALE, s_new], axis=-1)
    p = jax.nn.softmax(s, axis=-1).astype(ckv_new.dtype)
    return (jnp.einsum('bhqk,bkc->bqhc', p[..., :P], ckv_past)
            + jnp.einsum('bhqk,bkc->bqhc', p[..., P:], ckv_new))


def mem_kv(mem, mem_norm_g, w_mem_kv):
    B, M, _ = mem.shape
    kv = rms_norm(mem, mem_norm_g) @ w_mem_kv
    k = kv[..., :MEM_W].reshape(B, M, MEM_HEADS, HEAD_DIM)
    v = kv[..., MEM_W:].reshape(B, M, MEM_HEADS, HEAD_DIM)
    return k, v


def mem_attend(q, mk, mv):
    B, T = q.shape[:2]
    s = jnp.einsum('bthd,bmhd->bhtm', q, mk).astype(jnp.float32) * MEM_SCALE
    p = jax.nn.softmax(s, axis=-1).astype(mv.dtype)
    return jnp.einsum('bhtm,bmhd->bthd', p, mv).reshape(B, T, MEM_W)


def merge(h, ret_o, mla_lat, mem_o, gate, ret_gn_g, w_uv, w_out):
    B, T, _ = h.shape
    ret = head_norm(ret_o, ret_gn_g)
    mla = jnp.einsum('bthc,chd->bthd', mla_lat, w_uv).reshape(B, T, MLA_W)
    cat = jnp.concatenate([ret, mla, mem_o], axis=-1) * jax.nn.silu(gate)
    return h + cat @ w_out


def setup_inputs(seed: int = 0) -> dict:
    key = jax.random.key(seed)
    ks = jax.random.split(key, 24)
    f32 = jnp.float32
    N_PAGES = PAST_LEN // PAGE_SIZE
    n_used = DEC_BATCH * N_PAGES
    N_PHYS = n_used + n_used // 4
    nrm = lambda k, shape, scale: jax.random.normal(k, shape, f32) * scale
    gain = lambda k, shape: 1.0 + 0.02 * jax.random.normal(k, shape, f32)
    perm = jax.random.permutation(ks[6], N_PHYS)
    page_table = perm[:n_used].reshape(DEC_BATCH, N_PAGES).astype(jnp.int32)
    return {
        'x_prompt': nrm(ks[0], (BATCH, SEQ, D_MODEL), 1.0),
        'x_sample': nrm(ks[1], (DEC_BATCH, DEC_SEQ, D_MODEL), 1.0),
        'mem_prompt': nrm(ks[2], (BATCH, N_MEM, D_MODEL), 1.0),
        'cache_ckv': nrm(ks[3], (DEPTH, N_PHYS, PAGE_SIZE, KV_LORA), 1.0),
        'cache_kpe': nrm(ks[4], (DEPTH, N_PHYS, PAGE_SIZE, MLA_ROPE), 1.0),
        'page_table': page_table,
        'state_ret': nrm(ks[5], (DEPTH, DEC_BATCH, RET_HEADS, HEAD_DIM, HEAD_DIM), 0.1),
        'cache_mem_k': nrm(ks[7], (DEPTH, DEC_BATCH, N_MEM, MEM_HEADS, HEAD_DIM), 1.0),
        'cache_mem_v': nrm(ks[8], (DEPTH, DEC_BATCH, N_MEM, MEM_HEADS, HEAD_DIM), 1.0),
        'norm_g': gain(ks[9], (DEPTH, D_MODEL)),
        'w_in': nrm(ks[10], (DEPTH, D_MODEL, D_IN), D_MODEL ** -0.5),
        'ret_gn_g': gain(ks[11], (DEPTH, RET_W)),
        'mla_qnorm_g': gain(ks[12], (DEPTH, Q_LORA)),
        'w_uq': nrm(ks[13], (DEPTH, Q_LORA, MLA_HEADS, MLA_NOPE + MLA_ROPE), Q_LORA ** -0.5),
        'w_uk': nrm(ks[14], (DEPTH, KV_LORA, MLA_HEADS, MLA_NOPE), KV_LORA ** -0.5),
        'mla_kvnorm_g': gain(ks[15], (DEPTH, KV_LORA)),
        'w_uv': nrm(ks[16], (DEPTH, KV_LORA, MLA_HEADS, MLA_V), KV_LORA ** -0.5),
        'mem_norm_g': gain(ks[17], (DEPTH, D_MODEL)),
        'w_mem_kv': nrm(ks[18], (DEPTH, D_MODEL, 2 * MEM_W), D_MODEL ** -0.5),
        'w_out': nrm(ks[19], (DEPTH, D_MIX, D_MODEL), D_MIX ** -0.5),
        'final_norm_g': gain(ks[20], (D_MODEL,)),
    }


def reference(x_prompt, x_sample, mem_prompt, cache_ckv, cache_kpe, page_table, state_ret, cache_mem_k, cache_mem_v,
              norm_g, w_in, ret_gn_g, mla_qnorm_g, w_uq, w_uk, mla_kvnorm_g, w_uv, mem_norm_g, w_mem_kv, w_out,
              final_norm_g):
    T_p = x_prompt.shape[1]
    T_s = x_sample.shape[1]
    pos_p = jnp.arange(T_p, dtype=jnp.int32)
    pos_s = PAST_LEN + jnp.arange(T_s, dtype=jnp.int32)
    hp, hs = x_prompt, x_sample
    ckv_p_l, kpe_p_l, ret_p_l, mk_p_l, mv_p_l = [], [], [], [], []
    ckv_s_l, kpe_s_l, ret_s_l = [], [], []
    for l in range(DEPTH):
        rq, rk, rv, q_lat, q_pe, ckv, kpe, mq, gate = project(
            hp, pos_p, norm_g[l], w_in[l], mla_qnorm_g[l], w_uq[l], w_uk[l], mla_kvnorm_g[l])
        state0 = jnp.zeros((hp.shape[0], RET_HEADS, HEAD_DIM, HEAD_DIM), dtype=hp.dtype)
        ret_o, ret_state = retention(rq, rk, rv, state0)
        mla_lat = mla_prompt(q_lat, q_pe, ckv, kpe)
        mk, mv = mem_kv(mem_prompt, mem_norm_g[l], w_mem_kv[l])
        mem_o = mem_attend(mq, mk, mv)
        hp = merge(hp, ret_o, mla_lat, mem_o, gate, ret_gn_g[l], w_uv[l], w_out[l])
        ckv_p_l.append(ckv); kpe_p_l.append(kpe); ret_p_l.append(ret_state); mk_p_l.append(mk); mv_p_l.append(mv)
        rq, rk, rv, q_lat, q_pe, ckv, kpe, mq, gate = project(
            hs, pos_s, norm_g[l], w_in[l], mla_qnorm_g[l], w_uq[l], w_uk[l], mla_kvnorm_g[l])
        ret_o, ret_state = retention(rq, rk, rv, state_ret[l])
        mla_lat = mla_sample(q_lat, q_pe, ckv, kpe, cache_ckv[l], cache_kpe[l], page_table)
        mem_o = mem_attend(mq, cache_mem_k[l], cache_mem_v[l])
        hs = merge(hs, ret_o, mla_lat, mem_o, gate, ret_gn_g[l], w_uv[l], w_out[l])
        ckv_s_l.append(ckv); kpe_s_l.append(kpe); ret_s_l.append(ret_state)
    y_prompt = rms_norm(hp, final_norm_g)
    y_sample = rms_norm(hs, final_norm_g)
    return (y_prompt, y_sample,
            jnp.stack(ckv_p_l), jnp.stack(kpe_p_l), jnp.stack(ret_p_l), jnp.stack(mk_p_l), jnp.stack(mv_p_l),
            jnp.stack(ckv_s_l), jnp.stack(kpe_s_l), jnp.stack(ret_s_l))
```

```python
import functools

import jax
import jax.numpy as jnp
from jax import lax
from jax.experimental import pallas as pl
from jax.experimental.pallas import tpu as pltpu

F32 = jnp.float32
BF16 = jnp.bfloat16

HEAD_DIM = 128
N_HEADS = 4
RET_W = N_HEADS * HEAD_DIM
MLA_NOPE = 128
MLA_ROPE = 64
MLA_V = 128
Q_LORA = 384
KV_LORA = 256
MEM_W = N_HEADS * HEAD_DIM
D_MIX = 3 * RET_W
RET_CHUNK = 128
PAST_LEN = 16384
ROPE_BASE = 10000.0
EPS = 1e-6
MLA_SCALE = (MLA_NOPE + MLA_ROPE) ** -0.5
MEM_SCALE = HEAD_DIM ** -0.5
RK_SCALE = HEAD_DIM ** -0.5
NEG_BIG = -1e30

LANES = 128
QK_W = KV_LORA + LANES
OFF_RQ, OFF_RK, OFF_RV = 0, RET_W, 2 * RET_W
OFF_CQ = 3 * RET_W
OFF_CKV = OFF_CQ + Q_LORA
OFF_MQ = OFF_CKV + KV_LORA
OFF_GATE = OFF_MQ + MEM_W
OFF_KPE = OFF_GATE + D_MIX
D_IN2 = OFF_KPE + LANES

VMEM_LIMIT = 48 * 1024 * 1024


def _cparams(*sem):
    return pltpu.CompilerParams(dimension_semantics=sem, vmem_limit_bytes=VMEM_LIMIT)


def _rms(x, g):
    return x * lax.rsqrt(jnp.mean(x * x, axis=-1, keepdims=True) + EPS) * g


def _dot(a, b):
    return jnp.dot(a, b, preferred_element_type=F32)


def _dot_nt(a, b):
    return lax.dot_general(a, b, (((1,), (1,)), ((), ())), preferred_element_type=F32)


def _rot_half(x, cos, sin):
    return x * cos + pltpu.roll(x, 64, 1) * sin


def _proj_kernel(x_ref, cr_ref, sr_ref, cm_ref, sm_ref, g_ref, win_ref, gq_ref, wuq_ref, wuk_ref, gkv_ref,
                 rq_o, rk_o, rv_o, qa_o, kv_o, ckv_o, kpe_o, mq_o, sg_o):
    xn = _rms(x_ref[...], g_ref[...]).astype(BF16)
    cr, sr = cr_ref[...], sr_ref[...]
    cm, sm = cm_ref[...], sm_ref[...]

    zq = _dot(xn, win_ref[:, OFF_RQ:OFF_RQ + RET_W])
    zk = _dot(xn, win_ref[:, OFF_RK:OFF_RK + RET_W])
    for h in range(N_HEADS):
        sl = slice(h * HEAD_DIM, (h + 1) * HEAD_DIM)
        rq_o[:, sl] = _rot_half(zq[:, sl], cr, sr).astype(BF16)
        rk_o[:, sl] = (_rot_half(zk[:, sl], cr, sr) * RK_SCALE).astype(BF16)
    rv_o[...] = _dot(xn, win_ref[:, OFF_RV:OFF_RV + RET_W]).astype(BF16)

    cq = _rms(_dot(xn, win_ref[:, OFF_CQ:OFF_CQ + Q_LORA]), gq_ref[...]).astype(BF16)
    q = _dot(cq, wuq_ref[...])
    for h in range(N_HEADS):
        qn = q[:, h * MLA_NOPE:(h + 1) * MLA_NOPE].astype(BF16)
        qa_o[:, h * QK_W:h * QK_W + KV_LORA] = _dot(qn, wuk_ref[h]).astype(BF16)
        qp = q[:, RET_W + h * LANES:RET_W + (h + 1) * LANES]
        qa_o[:, h * QK_W + KV_LORA:(h + 1) * QK_W] = _rot_half(qp, cm, sm).astype(BF16)

    ckv = _rms(_dot(xn, win_ref[:, OFF_CKV:OFF_CKV + KV_LORA]), gkv_ref[...])
    ckv_o[...] = ckv
    kv_o[:, 0:KV_LORA] = ckv.astype(BF16)
    kp = _rot_half(_dot(xn, win_ref[:, OFF_KPE:OFF_KPE + LANES]), cm, sm)
    kv_o[:, KV_LORA:QK_W] = kp.astype(BF16)
    kpe_o[...] = (kp + pltpu.roll(kp, 96, 1))[:, 0:MLA_ROPE]

    mq_o[...] = _dot(xn, win_ref[:, OFF_MQ:OFF_MQ + MEM_W]).astype(BF16)
    gate = _dot(xn, win_ref[:, OFF_GATE:OFF_GATE + D_MIX])
    sg_o[...] = (gate / (1.0 + jnp.exp(-gate))).astype(BF16)


def _project(x2, tabs, g, win, gq, wuq, wuk, gkv, tm):
    n, d = x2.shape
    n_tab = tabs[0].shape[0] // tm
    row = lambda w: pl.BlockSpec((tm, w), lambda i: (i, 0))
    tab = pl.BlockSpec((tm, LANES), lambda i: (i % n_tab, 0))
    full = lambda a: pl.BlockSpec(a.shape, lambda i: (0,) * a.ndim)
    outs = [(RET_W, BF16), (RET_W, BF16), (RET_W, BF16), (N_HEADS * QK_W, BF16), (QK_W, BF16),
            (KV_LORA, F32), (MLA_ROPE, F32), (MEM_W, BF16), (D_MIX, BF16)]
    return pl.pallas_call(
        _proj_kernel,
        grid=(n // tm,),
        in_specs=[row(d), tab, tab, tab, tab, full(g), full(win), full(gq), full(wuq), full(wuk), full(gkv)],
        out_specs=[row(w) for w, _ in outs],
        out_shape=[jax.ShapeDtypeStruct((n, w), dt) for w, dt in outs],
        compiler_params=_cparams("parallel"),
        name="proj",
    )(x2, *tabs, g, win, gq, wuq, wuk, gkv)


def _ret_kernel(rq_ref, rk_ref, rv_ref, intra_ref, qdec_ref, kdec_ref, sdec_ref, gn_ref,
                ret_o, state_o, s_scr):
    c = pl.program_id(1)

    @pl.when(c == 0)
    def _():
        s_scr[...] = jnp.zeros_like(s_scr)

    for h in range(N_HEADS):
        sl = slice(h * HEAD_DIM, (h + 1) * HEAD_DIM)
        q, k, v = rq_ref[:, sl], rk_ref[:, sl], rv_ref[:, sl]
        s_old = s_scr[h]
        sc = _dot_nt(q, k) * intra_ref[h]
        o = _dot(sc.astype(BF16), v) + qdec_ref[h] * _dot(q, s_old.astype(BF16))
        kd_t = jnp.transpose(k.astype(F32) * kdec_ref[h]).astype(BF16)
        s_scr[h] = s_old * sdec_ref[h] + _dot(kd_t, v)
        mu = jnp.mean(o, axis=-1, keepdims=True)
        d = o - mu
        var = jnp.mean(d * d, axis=-1, keepdims=True)
        ret_o[:, sl] = (d * lax.rsqrt(var + EPS) * gn_ref[:, sl]).astype(BF16)

    @pl.when(c == pl.num_programs(1) - 1)
    def _():
        state_o[...] = s_scr[...]


def _ret_tables(chunk):
    log_g = jnp.log1p(-jnp.exp2(-5.0 - jnp.arange(N_HEADS, dtype=F32)))
    idx = jnp.arange(chunk, dtype=F32)
    diff = idx[:, None] - idx[None, :]
    intra = jnp.where(diff[None] >= 0, jnp.exp(jnp.maximum(diff, 0.0)[None] * log_g[:, None, None]), 0.0)
    q_dec = jnp.exp((idx[None, :] + 1.0) * log_g[:, None])
    k_dec = jnp.exp((chunk - 1.0 - idx)[None, :] * log_g[:, None])
    s_dec = jnp.exp(chunk * log_g)
    bc = lambda a: jnp.broadcast_to(a[:, :, None], (N_HEADS, chunk, HEAD_DIM))
    return intra, bc(q_dec), bc(k_dec), jnp.broadcast_to(s_dec[:, None, None], (N_HEADS, 1, HEAD_DIM))


def _retention_prompt(rq, rk, rv, gn, batch, seq):
    chunk = RET_CHUNK
    nc = seq // chunk
    intra, qdec, kdec, sdec = _ret_tables(chunk)
    row = pl.BlockSpec((chunk, RET_W), lambda b, c: (b * nc + c, 0))
    full = lambda a: pl.BlockSpec(a.shape, lambda b, c: (0,) * a.ndim)
    return pl.pallas_call(
        _ret_kernel,
        grid=(batch, nc),
        in_specs=[row, row, row, full(intra), full(qdec), full(kdec), full(sdec), full(gn)],
        out_specs=[row, pl.BlockSpec((None, N_HEADS, HEAD_DIM, HEAD_DIM), lambda b, c: (b, 0, 0, 0))],
        out_shape=[jax.ShapeDtypeStruct((batch * seq, RET_W), BF16),
                   jax.ShapeDtypeStruct((batch, N_HEADS, HEAD_DIM, HEAD_DIM), F32)],
        scratch_shapes=[pltpu.VMEM((N_HEADS, HEAD_DIM, HEAD_DIM), F32)],
        compiler_params=_cparams("parallel", "arbitrary"),
        name="ret_prompt",
    )(rq, rk, rv, intra, qdec, kdec, sdec, gn)


def _mla_kernel(qa_ref, kv_ref, wuv_ref, mla_o, q_scr, m_scr, l_scr, acc_scr, *, tq):
    i = pl.program_id(1)
    rows = N_HEADS * tq
    for h in range(N_HEADS):
        q_scr[h * tq:(h + 1) * tq, :] = qa_ref[:, h * QK_W:(h + 1) * QK_W]
    m_scr[...] = jnp.full_like(m_scr, NEG_BIG)
    l_scr[...] = jnp.zeros_like(l_scr)
    acc_scr[...] = jnp.zeros_like(acc_scr)

    def block(j, masked):
        kj = kv_ref[pl.ds(pl.multiple_of(j * tq, tq), tq), :]
        s = _dot_nt(q_scr[...], kj) * MLA_SCALE
        if masked:
            qpos = jnp.bitwise_and(lax.broadcasted_iota(jnp.int32, (rows, tq), 0), tq - 1)
            kpos = lax.broadcasted_iota(jnp.int32, (rows, tq), 1)
            s = jnp.where(kpos <= qpos, s, NEG_BIG)
        m_old = m_scr[...]
        m_new = jnp.maximum(m_old, jnp.max(s, axis=-1, keepdims=True))
        alpha = jnp.exp(m_old - m_new)
        p = jnp.exp(s - m_new)
        l_scr[...] = alpha * l_scr[...] + jnp.sum(p, axis=-1, keepdims=True)
        acc_scr[...] = alpha * acc_scr[...] + _dot(p.astype(BF16), kj[:, 0:KV_LORA])
        m_scr[...] = m_new

    def body(j, carry):
        block(j, False)
        return carry

    lax.fori_loop(0, i, body, 0)
    block(i, True)
    lat = (acc_scr[...] / l_scr[...]).astype(BF16)
    for h in range(N_HEADS):
        mla_o[:, h * MLA_V:(h + 1) * MLA_V] = _dot(lat[h * tq:(h + 1) * tq, :], wuv_ref[h]).astype(BF16)


def _mla_prompt(qa, kv, wuv, batch, seq, tq):
    nq = seq // tq
    return pl.pallas_call(
        functools.partial(_mla_kernel, tq=tq),
        grid=(batch, nq),
        in_specs=[pl.BlockSpec((tq, N_HEADS * QK_W), lambda b, i: (b * nq + i, 0)),
                  pl.BlockSpec((seq, QK_W), lambda b, i: (b, 0)),
                  pl.BlockSpec(wuv.shape, lambda b, i: (0, 0, 0))],
        out_specs=pl.BlockSpec((tq, N_HEADS * MLA_V), lambda b, i: (b * nq + i, 0)),
        out_shape=jax.ShapeDtypeStruct((batch * seq, N_HEADS * MLA_V), BF16),
        scratch_shapes=[pltpu.VMEM((N_HEADS * tq, QK_W), BF16), pltpu.VMEM((N_HEADS * tq, 1), F32),
                        pltpu.VMEM((N_HEADS * tq, 1), F32), pltpu.VMEM((N_HEADS * tq, KV_LORA), F32)],
        compiler_params=_cparams("parallel", "arbitrary"),
        name="mla_prompt",
    )(qa, kv, wuv)


def _memkv_kernel(mem_ref, g_ref, w_ref, k_o, v_o):
    kvp = _dot(_rms(mem_ref[...], g_ref[...]).astype(BF16), w_ref[...])
    k_o[...] = kvp[:, 0:MEM_W]
    v_o[...] = kvp[:, MEM_W:2 * MEM_W]


def _mem_kv(mem2, g, w, tm):
    n, d = mem2.shape
    row = lambda wd: pl.BlockSpec((tm, wd), lambda i: (i, 0))
    full = lambda a: pl.BlockSpec(a.shape, lambda i: (0,) * a.ndim)
    return pl.pallas_call(
        _memkv_kernel,
        grid=(n // tm,),
        in_specs=[row(d), full(g), full(w)],
        out_specs=[row(MEM_W), row(MEM_W)],
        out_shape=[jax.ShapeDtypeStruct((n, MEM_W), F32)] * 2,
        compiler_params=_cparams("parallel"),
        name="mem_kv",
    )(mem2, g, w)


def _out_proj(x, cat_scr, wout_ref, gf_ref):
    return _rms(x + _dot(cat_scr[...], wout_ref[...]), gf_ref[...])


def _merge_prompt_kernel(x_ref, ret_ref, mla_ref, mq_ref, sg_ref, mk_ref, mv_ref, wout_ref, gf_ref,
                         y_o, cat_scr):
    cat_scr[:, 0:RET_W] = ret_ref[...] * sg_ref[:, 0:RET_W]
    cat_scr[:, RET_W:2 * RET_W] = mla_ref[...] * sg_ref[:, RET_W:2 * RET_W]
    for h in range(N_HEADS):
        sl = slice(h * HEAD_DIM, (h + 1) * HEAD_DIM)
        s = _dot_nt(mq_ref[:, sl], mk_ref[:, sl].astype(BF16)) * MEM_SCALE
        p = jnp.exp(s - jnp.max(s, axis=-1, keepdims=True))
        p = (p / jnp.sum(p, axis=-1, keepdims=True)).astype(BF16)
        o = _dot(p, mv_ref[:, sl].astype(BF16))
        gsl = slice(2 * RET_W + h * HEAD_DIM, 2 * RET_W + (h + 1) * HEAD_DIM)
        cat_scr[:, gsl] = (o * sg_ref[:, gsl].astype(F32)).astype(BF16)
    y_o[...] = _out_proj(x_ref[...], cat_scr, wout_ref, gf_ref)


def _merge_prompt(x2, ret_n, mla, mq, sg, mk, mv, wout, gf, batch, seq, tm):
    nt = seq // tm
    n_mem = mk.shape[0] // batch
    row = lambda w: pl.BlockSpec((tm, w), lambda b, t: (b * nt + t, 0))
    mem = pl.BlockSpec((n_mem, MEM_W), lambda b, t: (b, 0))
    full = lambda a: pl.BlockSpec(a.shape, lambda b, t: (0,) * a.ndim)
    return pl.pallas_call(
        _merge_prompt_kernel,
        grid=(batch, nt),
        in_specs=[row(x2.shape[1]), row(RET_W), row(RET_W), row(MEM_W), row(D_MIX), mem, mem, full(wout), full(gf)],
        out_specs=row(x2.shape[1]),
        out_shape=jax.ShapeDtypeStruct(x2.shape, F32),
        scratch_shapes=[pltpu.VMEM((tm, D_MIX), BF16)],
        compiler_params=_cparams("parallel", "arbitrary"),
        name="merge_prompt",
    )(x2, ret_n, mla, mq, sg, mk, mv, wout, gf)


def _merge_sample_kernel(x_ref, ret_ref, lat_ref, memo_ref, sg_ref, wuv_ref, wout_ref, gf_ref, y_o, cat_scr):
    cat_scr[:, 0:RET_W] = ret_ref[...] * sg_ref[:, 0:RET_W]
    for h in range(N_HEADS):
        lat = lat_ref[:, h * KV_LORA:(h + 1) * KV_LORA].astype(BF16)
        gsl = slice(RET_W + h * MLA_V, RET_W + (h + 1) * MLA_V)
        cat_scr[:, gsl] = (_dot(lat, wuv_ref[h]) * sg_ref[:, gsl].astype(F32)).astype(BF16)
    cat_scr[:, 2 * RET_W:D_MIX] = (memo_ref[...] * sg_ref[:, 2 * RET_W:D_MIX].astype(F32)).astype(BF16)
    y_o[...] = _out_proj(x_ref[...], cat_scr, wout_ref, gf_ref)


def _merge_sample(x2, ret_n, lat, memo, sg, wuv, wout, gf):
    args = (x2, ret_n, lat, memo, sg, wuv, wout, gf)
    return pl.pallas_call(
        _merge_sample_kernel,
        grid=(1,),
        in_specs=[pl.BlockSpec(a.shape, lambda i, nd=a.ndim: (0,) * nd) for a in args],
        out_specs=pl.BlockSpec(x2.shape, lambda i: (0, 0)),
        out_shape=jax.ShapeDtypeStruct(x2.shape, F32),
        scratch_shapes=[pltpu.VMEM((x2.shape[0], D_MIX), BF16)],
        compiler_params=_cparams("arbitrary"),
        name="merge_sample",
    )(*args)


def _ret_step_kernel(rq_ref, rk_ref, rv_ref, s_ref, gam_ref, gn_ref, ret_o, s_o, *, bt):
    sq = (HEAD_DIM, HEAD_DIM)
    for b in range(bt):
        for h in range(N_HEADS):
            sl = slice(h * HEAD_DIM, (h + 1) * HEAD_DIM)
            q = rq_ref[b:b + 1, sl].astype(F32)
            k = rk_ref[b:b + 1, sl].astype(F32)
            v = rv_ref[b:b + 1, sl].astype(F32)
            gam = gam_ref[h]
            s_old = s_ref[b, h]
            q_col = jnp.transpose(jnp.broadcast_to(q, sq))
            k_col = jnp.transpose(jnp.broadcast_to(k, sq))
            qk = jnp.sum(q * k, axis=-1, keepdims=True)
            o = qk * v + gam * jnp.sum(q_col * s_old, axis=0, keepdims=True)
            s_o[b, h] = s_old * gam + k_col * v
            mu = jnp.mean(o, axis=-1, keepdims=True)
            d = o - mu
            var = jnp.mean(d * d, axis=-1, keepdims=True)
            ret_o[b:b + 1, sl] = (d * lax.rsqrt(var + EPS) * gn_ref[:, sl]).astype(BF16)


def _retention_step(rq, rk, rv, state, gn, bt):
    n = rq.shape[0]
    log_g = jnp.log1p(-jnp.exp2(-5.0 - jnp.arange(N_HEADS, dtype=F32)))
    gam = jnp.broadcast_to(jnp.exp(log_g)[:, None, None], (N_HEADS, 1, HEAD_DIM))
    row = pl.BlockSpec((bt, RET_W), lambda i: (i, 0))
    st = pl.BlockSpec((bt, N_HEADS, HEAD_DIM, HEAD_DIM), lambda i: (i, 0, 0, 0))
    full = lambda a: pl.BlockSpec(a.shape, lambda i: (0,) * a.ndim)
    return pl.pallas_call(
        functools.partial(_ret_step_kernel, bt=bt),
        grid=(n // bt,),
        in_specs=[row, row, row, st, full(gam), full(gn)],
        out_specs=[row, st],
        out_shape=[jax.ShapeDtypeStruct((n, RET_W), BF16), jax.ShapeDtypeStruct(state.shape, F32)],
        compiler_params=_cparams("parallel"),
        name="ret_step",
    )(rq, rk, rv, state, gam, gn)


def _mla_dec_kernel(pt_ref, q_ref, kvn_ref, *refs, pps):
    ckv_refs, kpe_refs = refs[:pps], refs[pps:2 * pps]
    o_ref, m_scr, l_scr, acc_scr = refs[2 * pps:]
    j = pl.program_id(1)

    @pl.when(j == 0)
    def _():
        m_scr[...] = jnp.full_like(m_scr, NEG_BIG)
        l_scr[...] = jnp.zeros_like(l_scr)
        acc_scr[...] = jnp.zeros_like(acc_scr)

    q = q_ref[...]
    ql, qp = q[:, 0:KV_LORA], q[:, KV_LORA:KV_LORA + MLA_ROPE]

    def update(s_list, v_list):
        m_old = m_scr[...]
        m_new = m_old
        for s in s_list:
            m_new = jnp.maximum(m_new, jnp.max(s, axis=-1, keepdims=True))
        alpha = jnp.exp(m_old - m_new)
        l = alpha * l_scr[...]
        acc = alpha * acc_scr[...]
        for s, v in zip(s_list, v_list):
            p = jnp.exp(s - m_new)
            l = l + jnp.sum(p, axis=-1, keepdims=True)
            acc = acc + _dot(p.astype(BF16), v)
        m_scr[...] = m_new
        l_scr[...] = l
        acc_scr[...] = acc

    s_list, v_list = [], []
    for r in range(pps):
        kc = ckv_refs[r][...].astype(BF16)
        kp = kpe_refs[r][...].astype(BF16)
        s_list.append((_dot_nt(ql, kc) + _dot_nt(qp, kp)) * MLA_SCALE)
        v_list.append(kc)
    update(s_list, v_list)

    @pl.when(j == pl.num_programs(1) - 1)
    def _():
        kvn = kvn_ref[...].astype(BF16)
        s = (_dot_nt(ql, kvn[:, 0:KV_LORA]) + _dot_nt(qp, kvn[:, KV_LORA:KV_LORA + MLA_ROPE])) * MLA_SCALE
        s = jnp.where(lax.broadcasted_iota(jnp.int32, s.shape, 1) == 0, s, NEG_BIG)
        update([s], [kvn[:, 0:KV_LORA]])
        o_ref[...] = acc_scr[...] / l_scr[...]


def _mla_decode(q8, kvn8, pool_ckv, pool_kpe, page_table, pps):
    n, n_pages = page_table.shape
    page = pool_ckv.shape[1]
    qw = q8.shape[-1]
    spec_q = pl.BlockSpec((None, 8, qw), lambda b, j, pt: (b, 0, 0))
    page_spec = lambda w, r: pl.BlockSpec((None, page, w), lambda b, j, pt: (pt[b, j * pps + r], 0, 0))
    grid_spec = pltpu.PrefetchScalarGridSpec(
        num_scalar_prefetch=1,
        grid=(n, n_pages // pps),
        in_specs=[spec_q, spec_q] + [page_spec(KV_LORA, r) for r in range(pps)]
                 + [page_spec(MLA_ROPE, r) for r in range(pps)],
        out_specs=pl.BlockSpec((None, 8, KV_LORA), lambda b, j, pt: (b, 0, 0)),
        scratch_shapes=[pltpu.VMEM((8, 1), F32), pltpu.VMEM((8, 1), F32), pltpu.VMEM((8, KV_LORA), F32)],
    )
    return pl.pallas_call(
        functools.partial(_mla_dec_kernel, pps=pps),
        grid_spec=grid_spec,
        out_shape=jax.ShapeDtypeStruct((n, 8, KV_LORA), F32),
        compiler_params=_cparams("parallel", "arbitrary"),
        name="mla_decode",
    )(page_table, q8, kvn8, *([pool_ckv] * pps), *([pool_kpe] * pps))


def _mem_dec_kernel(q_ref, mk_ref, mv_ref, o_ref, *, bt):
    for b in range(bt):
        for h in range(N_HEADS):
            sl = slice(h * HEAD_DIM, (h + 1) * HEAD_DIM)
            q8 = jnp.broadcast_to(q_ref[b:b + 1, sl], (8, HEAD_DIM))
            s = _dot_nt(q8, mk_ref[b, :, sl].astype(BF16)) * MEM_SCALE
            p = jnp.exp(s - jnp.max(s, axis=-1, keepdims=True))
            p = (p / jnp.sum(p, axis=-1, keepdims=True)).astype(BF16)
            o_ref[b:b + 1, sl] = _dot(p, mv_ref[b, :, sl].astype(BF16))[0:1, :]


def _mem_decode(mq, mk, mv, bt):
    n, n_mem, _ = mk.shape
    row = pl.BlockSpec((bt, MEM_W), lambda i: (i, 0))
    mem = pl.BlockSpec((bt, n_mem, MEM_W), lambda i: (i, 0, 0))
    return pl.pallas_call(
        functools.partial(_mem_dec_kernel, bt=bt),
        grid=(n // bt,),
        in_specs=[row, mem, mem],
        out_specs=row,
        out_shape=jax.ShapeDtypeStruct((n, MEM_W), F32),
        compiler_params=_cparams("parallel"),
        name="mem_decode",
    )(mq, mk, mv)


def _rope_tables(pos, n_freq, slot):
    inv = ROPE_BASE ** (-jnp.arange(0, 2 * n_freq, 2, dtype=F32) / (2 * n_freq))
    ang = pos.astype(F32)[:, None] * inv[None, :]
    c, s = jnp.cos(ang), jnp.sin(ang)
    z = jnp.zeros((pos.shape[0], slot // 2 - n_freq), F32)
    return jnp.concatenate([c, z, c, z], axis=1), jnp.concatenate([-s, z, s, z], axis=1)


def _spread(w):
    z = jnp.zeros(w.shape[:-1] + (32,), w.dtype)
    return jnp.concatenate([w[..., 0:32], z, w[..., 32:64], z], axis=-1)


def _prep_weights(w_in, w_uq, w_uk, w_uv, w_mem_kv, w_out):
    o = [0, 512, 1024, 1536, 1920, 2176, 2240, 2752, 4288]
    seg = lambda i: w_in[:, o[i]:o[i + 1]]
    win = jnp.concatenate([seg(0), seg(1), seg(2), seg(3), seg(4), seg(6), seg(7), _spread(seg(5))], axis=1)
    wuq = jnp.concatenate([w_uq[:, :, :MLA_NOPE].reshape(Q_LORA, -1),
                           _spread(w_uq[:, :, MLA_NOPE:]).reshape(Q_LORA, -1)], axis=1)
    wuk = jnp.transpose(w_uk, (1, 2, 0))
    wuv = jnp.transpose(w_uv, (1, 0, 2))
    return tuple(a.astype(BF16) for a in (win, wuq, wuk, wuv, w_mem_kv, w_out))


def kernel(x_prompt, x_sample, mem_prompt, cache_ckv, cache_kpe, page_table, state_ret, cache_mem_k, cache_mem_v,
           norm_g, w_in, ret_gn_g, mla_qnorm_g, w_uq, w_uk, mla_kvnorm_g, w_uv, mem_norm_g, w_mem_kv, w_out,
           final_norm_g):
    batch, seq, d_model = x_prompt.shape
    n_dec = x_sample.shape[0]
    n_mem = mem_prompt.shape[1]
    depth = w_in.shape[0]
    assert depth == 1 and x_sample.shape[1] == 1
    l = 0
    win, wuq, wuk, wuv, wmem, wout = _prep_weights(w_in[l], w_uq[l], w_uk[l], w_uv[l], w_mem_kv[l], w_out[l])
    g_in = norm_g[l][None, :]
    g_q = mla_qnorm_g[l][None, :]
    g_kv = mla_kvnorm_g[l][None, :]
    g_gn = ret_gn_g[l][None, :]
    g_mem = mem_norm_g[l][None, :]
    g_fin = final_norm_g[None, :]

    tm = min(512, seq)
    tq = min(256, seq)

    xp = x_prompt.reshape(batch * seq, d_model)
    pos_p = jnp.arange(seq, dtype=jnp.int32)
    tabs_p = _rope_tables(pos_p, HEAD_DIM // 2, LANES) + _rope_tables(pos_p, MLA_ROPE // 2, LANES)
    rq, rk, rv, qa, kvb, ckv, kpe, mq, sg = _project(xp, tabs_p, g_in, win, g_q, wuq, wuk, g_kv, tm)
    ret_n, ret_state_p = _retention_prompt(rq, rk, rv, g_gn, batch, seq)
    mla = _mla_prompt(qa, kvb, wuv, batch, seq, tq)
    mk, mv = _mem_kv(mem_prompt.reshape(batch * n_mem, d_model), g_mem, wmem, min(512, batch * n_mem))
    y_p = _merge_prompt(xp, ret_n, mla, mq, sg, mk, mv, wout, g_fin, batch, seq, tm)

    xs = x_sample.reshape(n_dec, d_model)
    pos_s = jnp.full((n_dec,), PAST_LEN, dtype=jnp.int32)
    tabs_s = _rope_tables(pos_s, HEAD_DIM // 2, LANES) + _rope_tables(pos_s, MLA_ROPE // 2, LANES)
    rq_s, rk_s, rv_s, qa_s, _, ckv_s, kpe_s, mq_s, sg_s = _project(xs, tabs_s, g_in, win, g_q, wuq, wuk, g_kv, n_dec)
    ret_n_s, ret_state_s = _retention_step(rq_s, rk_s, rv_s, state_ret[l], g_gn, 8)
    qh = qa_s.reshape(n_dec, N_HEADS, QK_W)
    q_std = jnp.concatenate([qh[..., 0:KV_LORA], qh[..., KV_LORA:KV_LORA + 32], qh[..., KV_LORA + 64:KV_LORA + 96]], -1)
    q8 = jnp.pad(q_std, ((0, 0), (0, 8 - N_HEADS), (0, 0)))
    kvn8 = jnp.pad(jnp.concatenate([ckv_s, kpe_s], axis=-1)[:, None, :], ((0, 0), (0, 7), (0, 0)))
    n_pages = page_table.shape[1]
    lat_s = _mla_decode(q8, kvn8, cache_ckv[l], cache_kpe[l], page_table, min(8, n_pages))
    lat_s = lat_s[:, 0:N_HEADS, :].reshape(n_dec, N_HEADS * KV_LORA)
    memo_s = _mem_decode(mq_s, cache_mem_k[l].reshape(n_dec, n_mem, MEM_W), cache_mem_v[l].reshape(n_dec, n_mem, MEM_W), 8)
    y_s = _merge_sample(xs, ret_n_s, lat_s, memo_s, sg_s, wuv, wout, g_fin)

    return (y_p.reshape(batch, seq, d_model), y_s.reshape(n_dec, 1, d_model),
            ckv.reshape(1, batch, seq, KV_LORA), kpe.reshape(1, batch, seq, MLA_ROPE),
            ret_state_p[None], mk.reshape(1, batch, n_mem, N_HEADS, HEAD_DIM), mv.reshape(1, batch, n_mem, N_HEADS, HEAD_DIM),
            ckv_s.reshape(1, n_dec, 1, KV_LORA), kpe_s.reshape(1, n_dec, 1, MLA_ROPE), ret_state_s[None])
```

```python
import functools

import jax
import jax.numpy as jnp
from jax import lax
from jax.experimental import pallas as pl
from jax.experimental.pallas import tpu as pltpu

F32 = jnp.float32
BF16 = jnp.bfloat16

HEAD_DIM = 128
N_HEADS = 4
RET_W = N_HEADS * HEAD_DIM
MLA_NOPE = 128
MLA_ROPE = 64
MLA_V = 128
Q_LORA = 384
KV_LORA = 256
MEM_W = N_HEADS * HEAD_DIM
D_MIX = 3 * RET_W
RET_CHUNK = 128
PAST_LEN = 16384
ROPE_BASE = 10000.0
EPS = 1e-6
MLA_SCALE = (MLA_NOPE + MLA_ROPE) ** -0.5
MEM_SCALE = HEAD_DIM ** -0.5
RK_SCALE = HEAD_DIM ** -0.5
NEG_BIG = -1e30
LOG2E = 1.4426950408889634

LANES = 128
QK_W = KV_LORA + LANES
OFF_RQ, OFF_RK, OFF_RV = 0, RET_W, 2 * RET_W
OFF_CQ = 3 * RET_W
OFF_CKV = OFF_CQ + Q_LORA
OFF_MQ = OFF_CKV + KV_LORA
OFF_GATE = OFF_MQ + MEM_W
OFF_KPE = OFF_GATE + D_MIX
D_IN2 = OFF_KPE + LANES

VMEM_LIMIT = 48 * 1024 * 1024


def _cparams(*sem):
    return pltpu.CompilerParams(dimension_semantics=sem, vmem_limit_bytes=VMEM_LIMIT)


def _rms(x, g):
    return x * lax.rsqrt(jnp.mean(x * x, axis=-1, keepdims=True) + EPS) * g


def _dot(a, b):
    return jnp.dot(a, b, preferred_element_type=F32)


def _dot_nt(a, b):
    return lax.dot_general(a, b, (((1,), (1,)), ((), ())), preferred_element_type=F32)


def _rot_half(x, cos, sin):
    return x * cos + pltpu.roll(x, 64, 1) * sin


def _proj_kernel(x_ref, cr_ref, sr_ref, cm_ref, sm_ref, g_ref, win_ref, gq_ref, wuq_ref, wuk_ref, gkv_ref,
                 rq_o, rk_o, rv_o, qa_o, kv_o, ckv_o, mq_o, sg_o, kpet_o, vt_o):
    xn = _rms(x_ref[...], g_ref[...]).astype(BF16)
    cr, sr = cr_ref[...], sr_ref[...]
    cm, sm = cm_ref[...], sm_ref[...]

    zq = _dot(xn, win_ref[:, OFF_RQ:OFF_RQ + RET_W])
    zk = _dot(xn, win_ref[:, OFF_RK:OFF_RK + RET_W])
    for h in range(N_HEADS):
        sl = slice(h * HEAD_DIM, (h + 1) * HEAD_DIM)
        rq_o[:, sl] = _rot_half(zq[:, sl], cr, sr).astype(BF16)
        rk_o[:, sl] = (_rot_half(zk[:, sl], cr, sr) * RK_SCALE).astype(BF16)
    rv_o[...] = _dot(xn, win_ref[:, OFF_RV:OFF_RV + RET_W]).astype(BF16)

    cq = _rms(_dot(xn, win_ref[:, OFF_CQ:OFF_CQ + Q_LORA]), gq_ref[...]).astype(BF16)
    q = _dot(cq, wuq_ref[...])
    for h in range(N_HEADS):
        qn = q[:, h * MLA_NOPE:(h + 1) * MLA_NOPE].astype(BF16)
        qa_o[:, h * QK_W:h * QK_W + KV_LORA] = _dot(qn, wuk_ref[h]).astype(BF16)
        qp = q[:, RET_W + h * LANES:RET_W + (h + 1) * LANES]
        qa_o[:, h * QK_W + KV_LORA:(h + 1) * QK_W] = _rot_half(qp, cm, sm).astype(BF16)

    ckv = _rms(_dot(xn, win_ref[:, OFF_CKV:OFF_CKV + KV_LORA]), gkv_ref[...])
    ckv_o[...] = ckv
    kv_o[:, 0:KV_LORA] = ckv.astype(BF16)
    kp = _rot_half(_dot(xn, win_ref[:, OFF_KPE:OFF_KPE + LANES]), cm, sm)
    kv_o[:, KV_LORA:QK_W] = kp.astype(BF16)
    kpt = jnp.transpose(kp)
    kpet_o[0:32, :] = kpt[0:32, :]
    kpet_o[32:64, :] = kpt[64:96, :]
    vt_o[...] = jnp.transpose(ckv).astype(BF16)

    mq_o[...] = _dot(xn, win_ref[:, OFF_MQ:OFF_MQ + MEM_W]).astype(BF16)
    gate = _dot(xn, win_ref[:, OFF_GATE:OFF_GATE + D_MIX])
    sg_o[...] = (gate / (1.0 + jnp.exp(-gate))).astype(BF16)


def _project(x2, tabs, g, win, gq, wuq, wuk, gkv, batch, seq, tm):
    n, d = x2.shape
    nt = seq // tm
    row = lambda w: pl.BlockSpec((tm, w), lambda i: (i, 0))
    tab = pl.BlockSpec((tm, LANES), lambda i: (i % nt, 0))
    full = lambda a: pl.BlockSpec(a.shape, lambda i: (0,) * a.ndim)
    outs = [(RET_W, BF16), (RET_W, BF16), (RET_W, BF16), (N_HEADS * QK_W, BF16), (QK_W, BF16),
            (KV_LORA, F32), (MEM_W, BF16), (D_MIX, BF16)]
    kpet_spec = pl.BlockSpec((None, MLA_ROPE, tm), lambda i: (i // nt, 0, i % nt))
    vt_spec = pl.BlockSpec((None, None, KV_LORA, tm), lambda i: (i // nt, i % nt, 0, 0))
    return pl.pallas_call(
        _proj_kernel,
        grid=(n // tm,),
        in_specs=[row(d), tab, tab, tab, tab, full(g), full(win), full(gq), full(wuq), full(wuk), full(gkv)],
        out_specs=[row(w) for w, _ in outs] + [kpet_spec, vt_spec],
        out_shape=[jax.ShapeDtypeStruct((n, w), dt) for w, dt in outs]
                  + [jax.ShapeDtypeStruct((batch, MLA_ROPE, seq), F32),
                     jax.ShapeDtypeStruct((batch, nt, KV_LORA, tm), BF16)],
        compiler_params=_cparams("parallel"),
        name="proj",
    )(x2, *tabs, g, win, gq, wuq, wuk, gkv)


def _ret_kernel(rq_ref, rk_ref, rv_ref, intra_ref, qdec_ref, kdec_ref, sdec_ref, gn_ref,
                ret_o, state_o, s_scr):
    c = pl.program_id(1)

    @pl.when(c == 0)
    def _():
        s_scr[...] = jnp.zeros_like(s_scr)

    chunk = intra_ref.shape[1]
    for ci in range(rq_ref.shape[0] // chunk):
        rs = slice(ci * chunk, (ci + 1) * chunk)
        for h in range(N_HEADS):
            sl = slice(h * HEAD_DIM, (h + 1) * HEAD_DIM)
            q, k, v = rq_ref[rs, sl], rk_ref[rs, sl], rv_ref[rs, sl]
            s_old = s_scr[h]
            sc = _dot_nt(q, k) * intra_ref[h]
            o = _dot(sc.astype(BF16), v) + qdec_ref[h] * _dot(q, s_old.astype(BF16))
            kd_t = jnp.transpose(k.astype(F32) * kdec_ref[h]).astype(BF16)
            s_scr[h] = s_old * sdec_ref[h] + _dot(kd_t, v)
            mu = jnp.mean(o, axis=-1, keepdims=True)
            d = o - mu
            var = jnp.mean(d * d, axis=-1, keepdims=True)
            ret_o[rs, sl] = (d * lax.rsqrt(var + EPS) * gn_ref[:, sl]).astype(BF16)

    @pl.when(c == pl.num_programs(1) - 1)
    def _():
        state_o[...] = s_scr[...]


def _ret_tables(chunk):
    log_g = jnp.log1p(-jnp.exp2(-5.0 - jnp.arange(N_HEADS, dtype=F32)))
    idx = jnp.arange(chunk, dtype=F32)
    diff = idx[:, None] - idx[None, :]
    intra = jnp.where(diff[None] >= 0, jnp.exp(jnp.maximum(diff, 0.0)[None] * log_g[:, None, None]), 0.0)
    q_dec = jnp.exp((idx[None, :] + 1.0) * log_g[:, None])
    k_dec = jnp.exp((chunk - 1.0 - idx)[None, :] * log_g[:, None])
    s_dec = jnp.exp(chunk * log_g)
    bc = lambda a: jnp.broadcast_to(a[:, :, None], (N_HEADS, chunk, HEAD_DIM))
    return intra, bc(q_dec), bc(k_dec), jnp.broadcast_to(s_dec[:, None, None], (N_HEADS, 1, HEAD_DIM))


def _retention_prompt(rq, rk, rv, gn, batch, seq, tr):
    chunk = RET_CHUNK
    assert seq % tr == 0 and tr % chunk == 0
    nc = seq // tr
    intra, qdec, kdec, sdec = _ret_tables(chunk)
    row = pl.BlockSpec((tr, RET_W), lambda b, c: (b * nc + c, 0))
    full = lambda a: pl.BlockSpec(a.shape, lambda b, c: (0,) * a.ndim)
    return pl.pallas_call(
        _ret_kernel,
        grid=(batch, nc),
        in_specs=[row, row, row, full(intra), full(qdec), full(kdec), full(sdec), full(gn)],
        out_specs=[row, pl.BlockSpec((None, N_HEADS, HEAD_DIM, HEAD_DIM), lambda b, c: (b, 0, 0, 0))],
        out_shape=[jax.ShapeDtypeStruct((batch * seq, RET_W), BF16),
                   jax.ShapeDtypeStruct((batch, N_HEADS, HEAD_DIM, HEAD_DIM), F32)],
        scratch_shapes=[pltpu.VMEM((N_HEADS, HEAD_DIM, HEAD_DIM), F32)],
        compiler_params=_cparams("parallel", "arbitrary"),
        name="ret_prompt",
    )(rq, rk, rv, intra, qdec, kdec, sdec, gn)


def _mla_kernel(qa_ref, kv_ref, vt_ref, wuv_ref, mla_o, m_scr, l_scr, acc_scr, *, tq, tk):
    i = pl.program_id(1)
    c2 = MLA_SCALE * LOG2E
    m_scr[...] = jnp.full_like(m_scr, NEG_BIG)
    l_scr[...] = jnp.zeros_like(l_scr)
    acc_scr[...] = jnp.zeros_like(acc_scr)

    def block(j, masked):
        kj = kv_ref[pl.ds(pl.multiple_of(j * tk, tk), tk), :]
        vtj = vt_ref[j]
        if masked:
            kpos = j * tk + lax.broadcasted_iota(jnp.int32, (tk, tq), 0)
            qpos = i * tq + lax.broadcasted_iota(jnp.int32, (tk, tq), 1)
            keep = kpos <= qpos
        for h in range(N_HEADS):
            s = _dot_nt(kj, qa_ref[:, h * QK_W:(h + 1) * QK_W])
            if masked:
                s = jnp.where(keep, s, NEG_BIG)
            m_old = m_scr[h]
            m_new = jnp.maximum(m_old, jnp.max(s, axis=0, keepdims=True))
            alpha = jnp.exp2((m_old - m_new) * c2)
            p = jnp.exp2((s - m_new) * c2)
            l_scr[h] = alpha * l_scr[h] + jnp.sum(p, axis=0, keepdims=True)
            acc_scr[h] = alpha * acc_scr[h] + _dot(vtj, p.astype(BF16))
            m_scr[h] = m_new

    n_full = (i * tq) // tk

    def body(j, carry):
        block(j, False)
        return carry

    lax.fori_loop(0, n_full, body, 0)
    block(n_full, True)
    for h in range(N_HEADS):
        lat = jnp.transpose(acc_scr[h] / l_scr[h]).astype(BF16)
        mla_o[:, h * MLA_V:(h + 1) * MLA_V] = _dot(lat, wuv_ref[h]).astype(BF16)


def _mla_prompt(qa, kv, vt, wuv, batch, seq, tq, tk):
    nq = seq // tq
    assert vt.shape == (batch, seq // tk, KV_LORA, tk) and tk % tq == 0
    return pl.pallas_call(
        functools.partial(_mla_kernel, tq=tq, tk=tk),
        grid=(batch, nq),
        in_specs=[pl.BlockSpec((tq, N_HEADS * QK_W), lambda b, i: (b * nq + i, 0)),
                  pl.BlockSpec((seq, QK_W), lambda b, i: (b, 0)),
                  pl.BlockSpec((None, seq // tk, KV_LORA, tk), lambda b, i: (b, 0, 0, 0)),
                  pl.BlockSpec(wuv.shape, lambda b, i: (0, 0, 0))],
        out_specs=pl.BlockSpec((tq, N_HEADS * MLA_V), lambda b, i: (b * nq + i, 0)),
        out_shape=jax.ShapeDtypeStruct((batch * seq, N_HEADS * MLA_V), BF16),
        scratch_shapes=[pltpu.VMEM((N_HEADS, 1, tq), F32), pltpu.VMEM((N_HEADS, 1, tq), F32),
                        pltpu.VMEM((N_HEADS, KV_LORA, tq), F32)],
        compiler_params=_cparams("parallel", "arbitrary"),
        name="mla_prompt",
    )(qa, kv, vt, wuv)


def _memkv_kernel(mem_ref, g_ref, w_ref, k_o, v_o):
    kvp = _dot(_rms(mem_ref[...], g_ref[...]).astype(BF16), w_ref[...])
    k_o[...] = kvp[:, 0:MEM_W]
    v_o[...] = kvp[:, MEM_W:2 * MEM_W]


def _mem_kv(mem2, g, w, tm):
    n, d = mem2.shape
    row = lambda wd: pl.BlockSpec((tm, wd), lambda i: (i, 0))
    full = lambda a: pl.BlockSpec(a.shape, lambda i: (0,) * a.ndim)
    return pl.pallas_call(
        _memkv_kernel,
        grid=(n // tm,),
        in_specs=[row(d), full(g), full(w)],
        out_specs=[row(MEM_W), row(MEM_W)],
        out_shape=[jax.ShapeDtypeStruct((n, MEM_W), F32)] * 2,
        compiler_params=_cparams("parallel"),
        name="mem_kv",
    )(mem2, g, w)


def _out_proj(x, cat_scr, wout_ref, gf_ref):
    return _rms(x + _dot(cat_scr[...], wout_ref[...]), gf_ref[...])


def _merge_prompt_kernel(x_ref, ret_ref, mla_ref, mq_ref, sg_ref, mk_ref, mv_ref, wout_ref, gf_ref,
                         y_o, cat_scr):
    cat_scr[:, 0:RET_W] = ret_ref[...] * sg_ref[:, 0:RET_W]
    cat_scr[:, RET_W:2 * RET_W] = mla_ref[...] * sg_ref[:, RET_W:2 * RET_W]
    for h in range(N_HEADS):
        sl = slice(h * HEAD_DIM, (h + 1) * HEAD_DIM)
        s = _dot_nt(mq_ref[:, sl], mk_ref[:, sl].astype(BF16)) * MEM_SCALE
        p = jnp.exp(s - jnp.max(s, axis=-1, keepdims=True))
        p = (p / jnp.sum(p, axis=-1, keepdims=True)).astype(BF16)
        o = _dot(p, mv_ref[:, sl].astype(BF16))
        gsl = slice(2 * RET_W + h * HEAD_DIM, 2 * RET_W + (h + 1) * HEAD_DIM)
        cat_scr[:, gsl] = (o * sg_ref[:, gsl].astype(F32)).astype(BF16)
    y_o[...] = _out_proj(x_ref[...], cat_scr, wout_ref, gf_ref)


def _merge_prompt(x2, ret_n, mla, mq, sg, mk, mv, wout, gf, batch, seq, tm):
    nt = seq // tm
    n_mem = mk.shape[0] // batch
    row = lambda w: pl.BlockSpec((tm, w), lambda b, t: (b * nt + t, 0))
    mem = pl.BlockSpec((n_mem, MEM_W), lambda b, t: (b, 0))
    full = lambda a: pl.BlockSpec(a.shape, lambda b, t: (0,) * a.ndim)
    return pl.pallas_call(
        _merge_prompt_kernel,
        grid=(batch, nt),
        in_specs=[row(x2.shape[1]), row(RET_W), row(RET_W), row(MEM_W), row(D_MIX), mem, mem, full(wout), full(gf)],
        out_specs=row(x2.shape[1]),
        out_shape=jax.ShapeDtypeStruct(x2.shape, F32),
        scratch_shapes=[pltpu.VMEM((tm, D_MIX), BF16)],
        compiler_params=_cparams("parallel", "arbitrary"),
        name="merge_prompt",
    )(x2, ret_n, mla, mq, sg, mk, mv, wout, gf)


def _merge_sample_kernel(x_ref, ret_ref, lat_ref, memo_ref, sg_ref, wuv_ref, wout_ref, gf_ref, y_o, cat_scr):
    cat_scr[:, 0:RET_W] = ret_ref[...] * sg_ref[:, 0:RET_W]
    for h in range(N_HEADS):
        lat = lat_ref[:, h * KV_LORA:(h + 1) * KV_LORA].astype(BF16)
        gsl = slice(RET_W + h * MLA_V, RET_W + (h + 1) * MLA_V)
        cat_scr[:, gsl] = (_dot(lat, wuv_ref[h]) * sg_ref[:, gsl].astype(F32)).astype(BF16)
    cat_scr[:, 2 * RET_W:D_MIX] = (memo_ref[...] * sg_ref[:, 2 * RET_W:D_MIX].astype(F32)).astype(BF16)
    y_o[...] = _out_proj(x_ref[...], cat_scr, wout_ref, gf_ref)


def _merge_sample(x2, ret_n, lat, memo, sg, wuv, wout, gf):
    args = (x2, ret_n, lat, memo, sg, wuv, wout, gf)
    return pl.pallas_call(
        _merge_sample_kernel,
        grid=(1,),
        in_specs=[pl.BlockSpec(a.shape, lambda i, nd=a.ndim: (0,) * nd) for a in args],
        out_specs=pl.BlockSpec(x2.shape, lambda i: (0, 0)),
        out_shape=jax.ShapeDtypeStruct(x2.shape, F32),
        scratch_shapes=[pltpu.VMEM((x2.shape[0], D_MIX), BF16)],
        compiler_params=_cparams("arbitrary"),
        name="merge_sample",
    )(*args)


def _ret_step_kernel(rq_ref, rk_ref, rv_ref, s_ref, gam_ref, gn_ref, ret_o, s_o, *, bt):
    sq = (HEAD_DIM, HEAD_DIM)
    for b in range(bt):
        for h in range(N_HEADS):
            sl = slice(h * HEAD_DIM, (h + 1) * HEAD_DIM)
            q = rq_ref[b:b + 1, sl].astype(F32)
            k = rk_ref[b:b + 1, sl].astype(F32)
            v = rv_ref[b:b + 1, sl].astype(F32)
            gam = gam_ref[h]
            s_old = s_ref[b, h]
            q_col = jnp.transpose(jnp.broadcast_to(q, sq))
            k_col = jnp.transpose(jnp.broadcast_to(k, sq))
            qk = jnp.sum(q * k, axis=-1, keepdims=True)
            o = qk * v + gam * jnp.sum(q_col * s_old, axis=0, keepdims=True)
            s_o[b, h] = s_old * gam + k_col * v
            mu = jnp.mean(o, axis=-1, keepdims=True)
            d = o - mu
            var = jnp.mean(d * d, axis=-1, keepdims=True)
            ret_o[b:b + 1, sl] = (d * lax.rsqrt(var + EPS) * gn_ref[:, sl]).astype(BF16)


def _retention_step(rq, rk, rv, state, gn, bt):
    n = rq.shape[0]
    log_g = jnp.log1p(-jnp.exp2(-5.0 - jnp.arange(N_HEADS, dtype=F32)))
    gam = jnp.broadcast_to(jnp.exp(log_g)[:, None, None], (N_HEADS, 1, HEAD_DIM))
    row = pl.BlockSpec((bt, RET_W), lambda i: (i, 0))
    st = pl.BlockSpec((bt, N_HEADS, HEAD_DIM, HEAD_DIM), lambda i: (i, 0, 0, 0))
    full = lambda a: pl.BlockSpec(a.shape, lambda i: (0,) * a.ndim)
    return pl.pallas_call(
        functools.partial(_ret_step_kernel, bt=bt),
        grid=(n // bt,),
        in_specs=[row, row, row, st, full(gam), full(gn)],
        out_specs=[row, st],
        out_shape=[jax.ShapeDtypeStruct((n, RET_W), BF16), jax.ShapeDtypeStruct(state.shape, F32)],
        compiler_params=_cparams("parallel"),
        name="ret_step",
    )(rq, rk, rv, state, gam, gn)


def _mla_dec_kernel(pt_ref, q_ref, kvn_ref, ckv_hbm, kpet_hbm, o_ref,
                    ckv_buf, kpe_buf, sems, m_scr, l_scr, acc_scr, *, cp, page):
    b, j = pl.program_id(0), pl.program_id(1)
    nb, nj = pl.num_programs(0), pl.num_programs(1)
    t = b * nj + j
    slot = t % 2

    def chunk_copies(bb, jj, sl, lookup=True):
        out = []
        for r in range(cp):
            pg = pt_ref[bb, jj * cp + r] if lookup else 0
            out.append(pltpu.make_async_copy(ckv_hbm.at[pg], ckv_buf.at[sl, r], sems.at[0, sl]))
            out.append(pltpu.make_async_copy(kpet_hbm.at[pg], kpe_buf.at[sl, :, r * page:(r + 1) * page],
                                             sems.at[1, sl]))
        return out

    @pl.when(t == 0)
    def _():
        for cpy in chunk_copies(b, j, slot):
            cpy.start()

    @pl.when(t + 1 < nb * nj)
    def _():
        wrap = j + 1 == nj
        for cpy in chunk_copies(jnp.where(wrap, b + 1, b), jnp.where(wrap, 0, j + 1), 1 - slot):
            cpy.start()

    @pl.when(j == 0)
    def _():
        m_scr[...] = jnp.full_like(m_scr, NEG_BIG)
        l_scr[...] = jnp.zeros_like(l_scr)
        acc_scr[...] = jnp.zeros_like(acc_scr)

    q = q_ref[...]
    ql, qp = q[:, 0:KV_LORA], q[:, KV_LORA:KV_LORA + MLA_ROPE]
    c2 = MLA_SCALE * LOG2E

    def update(s, v):
        m_old = m_scr[...]
        m_new = jnp.maximum(m_old, jnp.max(s, axis=-1, keepdims=True))
        alpha = jnp.exp2((m_old - m_new) * c2)
        p = jnp.exp2((s - m_new) * c2)
        l_scr[...] = alpha * l_scr[...] + jnp.sum(p, axis=-1, keepdims=True)
        acc_scr[...] = alpha * acc_scr[...] + _dot(p.astype(BF16), v)
        m_scr[...] = m_new

    for cpy in chunk_copies(b, j, slot, lookup=False):
        cpy.wait()
    kc = ckv_buf[slot].reshape(cp * page, KV_LORA).astype(BF16)
    kpt = kpe_buf[slot].astype(BF16)
    update(_dot_nt(ql, kc) + _dot(qp, kpt), kc)

    @pl.when(j == nj - 1)
    def _():
        kvn = kvn_ref[...].astype(BF16)
        s = _dot_nt(ql, kvn[:, 0:KV_LORA]) + _dot_nt(qp, kvn[:, KV_LORA:KV_LORA + MLA_ROPE])
        s = jnp.where(lax.broadcasted_iota(jnp.int32, s.shape, 1) == 0, s, NEG_BIG)
        update(s, kvn[:, 0:KV_LORA])
        o_ref[...] = acc_scr[...] / l_scr[...]


def _mla_decode(q8, kvn8, pool_ckv, pool_kpet, page_table, cp):
    n, n_pages = page_table.shape
    page = pool_ckv.shape[1]
    qw = q8.shape[-1]
    spec_q = pl.BlockSpec((None, 8, qw), lambda b, j, pt: (b, 0, 0))
    grid_spec = pltpu.PrefetchScalarGridSpec(
        num_scalar_prefetch=1,
        grid=(n, n_pages // cp),
        in_specs=[spec_q, spec_q, pl.BlockSpec(memory_space=pl.ANY), pl.BlockSpec(memory_space=pl.ANY)],
        out_specs=pl.BlockSpec((None, 8, KV_LORA), lambda b, j, pt: (b, 0, 0)),
        scratch_shapes=[pltpu.VMEM((2, cp, page, KV_LORA), F32), pltpu.VMEM((2, MLA_ROPE, cp * page), F32),
                        pltpu.SemaphoreType.DMA((2, 2)),
                        pltpu.VMEM((8, 1), F32), pltpu.VMEM((8, 1), F32), pltpu.VMEM((8, KV_LORA), F32)],
    )
    return pl.pallas_call(
        functools.partial(_mla_dec_kernel, cp=cp, page=page),
        grid_spec=grid_spec,
        out_shape=jax.ShapeDtypeStruct((n, 8, KV_LORA), F32),
        compiler_params=_cparams("arbitrary", "arbitrary"),
        name="mla_decode",
    )(page_table, q8, kvn8, pool_ckv, pool_kpet)


def _mem_dec_kernel(q_ref, mk_ref, mv_ref, o_ref, *, bt):
    n_col = mk_ref.shape[1]
    col_head = jnp.bitwise_and(lax.broadcasted_iota(jnp.int32, (8, n_col), 1), N_HEADS - 1)
    own = col_head == lax.broadcasted_iota(jnp.int32, (8, n_col), 0)
    pad = jnp.zeros((8 - N_HEADS, HEAD_DIM), BF16)
    for b in range(bt):
        q8 = jnp.concatenate([q_ref[b], pad], axis=0)
        s = jnp.where(own, _dot_nt(q8, mk_ref[b].astype(BF16)) * MEM_SCALE, NEG_BIG)
        p = jnp.exp(s - jnp.max(s, axis=-1, keepdims=True))
        p = (p / jnp.sum(p, axis=-1, keepdims=True)).astype(BF16)
        o_ref[b] = _dot(p, mv_ref[b].astype(BF16))[0:N_HEADS, :]


def _mem_decode(mq, mk, mv, bt):
    n, n_col, _ = mk.shape
    row = pl.BlockSpec((bt, N_HEADS, HEAD_DIM), lambda i: (i, 0, 0))
    mem = pl.BlockSpec((bt, n_col, HEAD_DIM), lambda i: (i, 0, 0))
    return pl.pallas_call(
        functools.partial(_mem_dec_kernel, bt=bt),
        grid=(n // bt,),
        in_specs=[row, mem, mem],
        out_specs=row,
        out_shape=jax.ShapeDtypeStruct((n, N_HEADS, HEAD_DIM), F32),
        compiler_params=_cparams("parallel"),
        name="mem_decode",
    )(mq, mk, mv)


def _rope_tables(pos, n_freq, slot):
    inv = ROPE_BASE ** (-jnp.arange(0, 2 * n_freq, 2, dtype=F32) / (2 * n_freq))
    ang = pos.astype(F32)[:, None] * inv[None, :]
    c, s = jnp.cos(ang), jnp.sin(ang)
    z = jnp.zeros((pos.shape[0], slot // 2 - n_freq), F32)
    return jnp.concatenate([c, z, c, z], axis=1), jnp.concatenate([-s, z, s, z], axis=1)


def _spread(w):
    z = jnp.zeros(w.shape[:-1] + (32,), w.dtype)
    return jnp.concatenate([w[..., 0:32], z, w[..., 32:64], z], axis=-1)


def _prep_weights(w_in, w_uq, w_uk, w_uv, w_mem_kv, w_out):
    o = [0, 512, 1024, 1536, 1920, 2176, 2240, 2752, 4288]
    seg = lambda i: w_in[:, o[i]:o[i + 1]]
    win = jnp.concatenate([seg(0), seg(1), seg(2), seg(3), seg(4), seg(6), seg(7), _spread(seg(5))], axis=1)
    wuq = jnp.concatenate([w_uq[:, :, :MLA_NOPE].reshape(Q_LORA, -1),
                           _spread(w_uq[:, :, MLA_NOPE:]).reshape(Q_LORA, -1)], axis=1)
    wuk = jnp.transpose(w_uk, (1, 2, 0))
    wuv = jnp.transpose(w_uv, (1, 0, 2))
    return tuple(a.astype(BF16) for a in (win, wuq, wuk, wuv, w_mem_kv, w_out))


def kernel(x_prompt, x_sample, mem_prompt, cache_ckv, cache_kpe, page_table, state_ret, cache_mem_k, cache_mem_v,
           norm_g, w_in, ret_gn_g, mla_qnorm_g, w_uq, w_uk, mla_kvnorm_g, w_uv, mem_norm_g, w_mem_kv, w_out,
           final_norm_g):
    batch, seq, d_model = x_prompt.shape
    n_dec = x_sample.shape[0]
    n_mem = mem_prompt.shape[1]
    depth = w_in.shape[0]
    assert depth == 1 and x_sample.shape[1] == 1
    l = 0
    win, wuq, wuk, wuv, wmem, wout = _prep_weights(w_in[l], w_uq[l], w_uk[l], w_uv[l], w_mem_kv[l], w_out[l])
    g_in = norm_g[l][None, :]
    g_q = mla_qnorm_g[l][None, :]
    g_kv = mla_kvnorm_g[l][None, :]
    g_gn = ret_gn_g[l][None, :]
    g_mem = mem_norm_g[l][None, :]
    g_fin = final_norm_g[None, :]

    tm = min(512, seq)
    tq = min(256, seq)
    n_pages = page_table.shape[1]
    cp = min(32, n_pages)

    xp = x_prompt.reshape(batch * seq, d_model)
    pos_p = jnp.arange(seq, dtype=jnp.int32)
    tabs_p = _rope_tables(pos_p, HEAD_DIM // 2, LANES) + _rope_tables(pos_p, MLA_ROPE // 2, LANES)
    rq, rk, rv, qa, kvb, ckv, mq, sg, kpet, vt = _project(xp, tabs_p, g_in, win, g_q, wuq, wuk, g_kv, batch, seq, tm)
    ret_n, ret_state_p = _retention_prompt(rq, rk, rv, g_gn, batch, seq, tm)
    mla = _mla_prompt(qa, kvb, vt, wuv, batch, seq, tq, tm)
    mk, mv = _mem_kv(mem_prompt.reshape(batch * n_mem, d_model), g_mem, wmem, min(512, batch * n_mem))
    y_p = _merge_prompt(xp, ret_n, mla, mq, sg, mk, mv, wout, g_fin, batch, seq, tm)

    xs = x_sample.reshape(n_dec, d_model)
    pos_s = jnp.full((n_dec,), PAST_LEN, dtype=jnp.int32)
    tabs_s = _rope_tables(pos_s, HEAD_DIM // 2, LANES) + _rope_tables(pos_s, MLA_ROPE // 2, LANES)
    rq_s, rk_s, rv_s, qa_s, _, ckv_s, mq_s, sg_s, kpet_s, _ = _project(
        xs, tabs_s, g_in, win, g_q, wuq, wuk, g_kv, 1, n_dec, n_dec)
    kpe_s = jnp.swapaxes(kpet_s[0], 0, 1)
    ret_n_s, ret_state_s = _retention_step(rq_s, rk_s, rv_s, state_ret[l], g_gn, 8)
    qh = qa_s.reshape(n_dec, N_HEADS, QK_W)
    q_std = jnp.concatenate([qh[..., 0:KV_LORA], qh[..., KV_LORA:KV_LORA + 32], qh[..., KV_LORA + 64:KV_LORA + 96]], -1)
    q8 = jnp.pad(q_std, ((0, 0), (0, 8 - N_HEADS), (0, 0)))
    kvn8 = jnp.pad(jnp.concatenate([ckv_s, kpe_s], axis=-1)[:, None, :], ((0, 0), (0, 7), (0, 0)))
    lat_s = _mla_decode(q8, kvn8, cache_ckv[l], jnp.swapaxes(cache_kpe[l], 1, 2), page_table, cp)
    lat_s = lat_s[:, 0:N_HEADS, :].reshape(n_dec, N_HEADS * KV_LORA)
    memo_s = _mem_decode(mq_s.reshape(n_dec, N_HEADS, HEAD_DIM),
                         cache_mem_k[l].reshape(n_dec, n_mem * N_HEADS, HEAD_DIM),
                         cache_mem_v[l].reshape(n_dec, n_mem * N_HEADS, HEAD_DIM), 8).reshape(n_dec, MEM_W)
    y_s = _merge_sample(xs, ret_n_s, lat_s, memo_s, sg_s, wuv, wout, g_fin)

    return (y_p.reshape(batch, seq, d_model), y_s.reshape(n_dec, 1, d_model),
            ckv.reshape(1, batch, seq, KV_LORA), jnp.swapaxes(kpet, 1, 2)[None],
            ret_state_p[None], mk.reshape(1, batch, n_mem, N_HEADS, HEAD_DIM), mv.reshape(1, batch, n_mem, N_HEADS, HEAD_DIM),
            ckv_s.reshape(1, n_dec, 1, KV_LORA), kpe_s.reshape(1, n_dec, 1, MLA_ROPE), ret_state_s[None])
```

```python
import functools

import jax
import jax.numpy as jnp
from jax import lax
from jax.experimental import pallas as pl
from jax.experimental.pallas import tpu as pltpu

F32 = jnp.float32
BF16 = jnp.bfloat16

HEAD_DIM = 128
N_HEADS = 4
RET_W = N_HEADS * HEAD_DIM
MLA_NOPE = 128
MLA_ROPE = 64
MLA_V = 128
Q_LORA = 384
KV_LORA = 256
MEM_W = N_HEADS * HEAD_DIM
D_MIX = 3 * RET_W
RET_CHUNK = 128
PAST_LEN = 16384
ROPE_BASE = 10000.0
EPS = 1e-6
MLA_SCALE = (MLA_NOPE + MLA_ROPE) ** -0.5
MEM_SCALE = HEAD_DIM ** -0.5
RK_SCALE = HEAD_DIM ** -0.5
NEG_BIG = -1e30
LOG2E = 1.4426950408889634

LANES = 128
QK_W = KV_LORA + LANES
OFF_RQ, OFF_RK, OFF_RV = 0, RET_W, 2 * RET_W
OFF_CQ = 3 * RET_W
OFF_CKV = OFF_CQ + Q_LORA
OFF_MQ = OFF_CKV + KV_LORA
OFF_GATE = OFF_MQ + MEM_W
OFF_KPE = OFF_GATE + D_MIX
D_IN2 = OFF_KPE + LANES

VMEM_LIMIT = 48 * 1024 * 1024


def _cparams(*sem):
    return pltpu.CompilerParams(dimension_semantics=sem, vmem_limit_bytes=VMEM_LIMIT)


def _rms(x, g):
    return x * lax.rsqrt(jnp.mean(x * x, axis=-1, keepdims=True) + EPS) * g


def _dot(a, b):
    return jnp.dot(a, b, preferred_element_type=F32)


def _dot_nt(a, b):
    return lax.dot_general(a, b, (((1,), (1,)), ((), ())), preferred_element_type=F32)


def _rot_half(x, cos, sin):
    return x * cos + pltpu.roll(x, 64, 1) * sin


def _proj_kernel(x_ref, cr_ref, sr_ref, cm_ref, sm_ref, g_ref, win_ref, gq_ref, wuq_ref, wuk_ref, gkv_ref,
                 rq_o, rk_o, rv_o, qa_o, kv_o, ckv_o, mq_o, sg_o, kpet_o, vt_o):
    xn = _rms(x_ref[...], g_ref[...]).astype(BF16)
    cr, sr = cr_ref[...], sr_ref[...]
    cm, sm = cm_ref[...], sm_ref[...]

    zq = _dot(xn, win_ref[:, OFF_RQ:OFF_RQ + RET_W])
    zk = _dot(xn, win_ref[:, OFF_RK:OFF_RK + RET_W])
    for h in range(N_HEADS):
        sl = slice(h * HEAD_DIM, (h + 1) * HEAD_DIM)
        rq_o[:, sl] = _rot_half(zq[:, sl], cr, sr).astype(BF16)
        rk_o[:, sl] = (_rot_half(zk[:, sl], cr, sr) * RK_SCALE).astype(BF16)
    rv_o[...] = _dot(xn, win_ref[:, OFF_RV:OFF_RV + RET_W]).astype(BF16)

    cq = _rms(_dot(xn, win_ref[:, OFF_CQ:OFF_CQ + Q_LORA]), gq_ref[...]).astype(BF16)
    q = _dot(cq, wuq_ref[...])
    for h in range(N_HEADS):
        qn = q[:, h * MLA_NOPE:(h + 1) * MLA_NOPE].astype(BF16)
        qa_o[:, h * QK_W:h * QK_W + KV_LORA] = _dot(qn, wuk_ref[h]).astype(BF16)
        qp = q[:, RET_W + h * LANES:RET_W + (h + 1) * LANES]
        qa_o[:, h * QK_W + KV_LORA:(h + 1) * QK_W] = _rot_half(qp, cm, sm).astype(BF16)

    ckv = _rms(_dot(xn, win_ref[:, OFF_CKV:OFF_CKV + KV_LORA]), gkv_ref[...])
    ckv_o[...] = ckv
    kv_o[:, 0:KV_LORA] = ckv.astype(BF16)
    kp = _rot_half(_dot(xn, win_ref[:, OFF_KPE:OFF_KPE + LANES]), cm, sm)
    kv_o[:, KV_LORA:QK_W] = kp.astype(BF16)
    kpt = jnp.transpose(kp)
    kpet_o[0:32, :] = kpt[0:32, :]
    kpet_o[32:64, :] = kpt[64:96, :]
    vt = jnp.transpose(ckv).astype(BF16)
    tkb = vt_o.shape[-1]
    for u in range(vt_o.shape[0]):
        vt_o[u] = vt[:, u * tkb:(u + 1) * tkb]

    mq_o[...] = _dot(xn, win_ref[:, OFF_MQ:OFF_MQ + MEM_W]).astype(BF16)
    gate = _dot(xn, win_ref[:, OFF_GATE:OFF_GATE + D_MIX])
    sg_o[...] = (gate / (1.0 + jnp.exp(-gate))).astype(BF16)


def _project(x2, tabs, g, win, gq, wuq, wuk, gkv, batch, seq, tm, tkb):
    n, d = x2.shape
    nt = seq // tm
    row = lambda w: pl.BlockSpec((tm, w), lambda i: (i, 0))
    tab = pl.BlockSpec((tm, LANES), lambda i: (i % nt, 0))
    full = lambda a: pl.BlockSpec(a.shape, lambda i: (0,) * a.ndim)
    outs = [(RET_W, BF16), (RET_W, BF16), (RET_W, BF16), (N_HEADS * QK_W, BF16), (QK_W, BF16),
            (KV_LORA, F32), (MEM_W, BF16), (D_MIX, BF16)]
    kpet_spec = pl.BlockSpec((None, MLA_ROPE, tm), lambda i: (i // nt, 0, i % nt))
    vt_spec = pl.BlockSpec((None, tm // tkb, KV_LORA, tkb), lambda i: (i // nt, i % nt, 0, 0))
    return pl.pallas_call(
        _proj_kernel,
        grid=(n // tm,),
        in_specs=[row(d), tab, tab, tab, tab, full(g), full(win), full(gq), full(wuq), full(wuk), full(gkv)],
        out_specs=[row(w) for w, _ in outs] + [kpet_spec, vt_spec],
        out_shape=[jax.ShapeDtypeStruct((n, w), dt) for w, dt in outs]
                  + [jax.ShapeDtypeStruct((batch, MLA_ROPE, seq), F32),
                     jax.ShapeDtypeStruct((batch, seq // tkb, KV_LORA, tkb), BF16)],
        compiler_params=_cparams("parallel"),
        name="proj",
    )(x2, *tabs, g, win, gq, wuq, wuk, gkv)


def _ret_kernel(rq_ref, rk_ref, rv_ref, intra_ref, qdec_ref, kdec_ref, sdec_ref, gn_ref,
                ret_o, state_o, s_scr):
    c = pl.program_id(1)

    @pl.when(c == 0)
    def _():
        s_scr[...] = jnp.zeros_like(s_scr)

    chunk = intra_ref.shape[1]
    for ci in range(rq_ref.shape[0] // chunk):
        rs = slice(ci * chunk, (ci + 1) * chunk)
        for h in range(N_HEADS):
            sl = slice(h * HEAD_DIM, (h + 1) * HEAD_DIM)
            q, k, v = rq_ref[rs, sl], rk_ref[rs, sl], rv_ref[rs, sl]
            s_old = s_scr[h]
            sc = _dot_nt(q, k) * intra_ref[h]
            o = _dot(sc.astype(BF16), v) + qdec_ref[h] * _dot(q, s_old.astype(BF16))
            kd_t = jnp.transpose(k.astype(F32) * kdec_ref[h]).astype(BF16)
            s_scr[h] = s_old * sdec_ref[h] + _dot(kd_t, v)
            mu = jnp.mean(o, axis=-1, keepdims=True)
            d = o - mu
            var = jnp.mean(d * d, axis=-1, keepdims=True)
            ret_o[rs, sl] = (d * lax.rsqrt(var + EPS) * gn_ref[:, sl]).astype(BF16)

    @pl.when(c == pl.num_programs(1) - 1)
    def _():
        state_o[...] = s_scr[...]


def _ret_tables(chunk):
    log_g = jnp.log1p(-jnp.exp2(-5.0 - jnp.arange(N_HEADS, dtype=F32)))
    idx = jnp.arange(chunk, dtype=F32)
    diff = idx[:, None] - idx[None, :]
    intra = jnp.where(diff[None] >= 0, jnp.exp(jnp.maximum(diff, 0.0)[None] * log_g[:, None, None]), 0.0)
    q_dec = jnp.exp((idx[None, :] + 1.0) * log_g[:, None])
    k_dec = jnp.exp((chunk - 1.0 - idx)[None, :] * log_g[:, None])
    s_dec = jnp.exp(chunk * log_g)
    bc = lambda a: jnp.broadcast_to(a[:, :, None], (N_HEADS, chunk, HEAD_DIM))
    return intra, bc(q_dec), bc(k_dec), jnp.broadcast_to(s_dec[:, None, None], (N_HEADS, 1, HEAD_DIM))


def _retention_prompt(rq, rk, rv, gn, batch, seq, tr):
    chunk = RET_CHUNK
    assert seq % tr == 0 and tr % chunk == 0
    nc = seq // tr
    intra, qdec, kdec, sdec = _ret_tables(chunk)
    row = pl.BlockSpec((tr, RET_W), lambda b, c: (b * nc + c, 0))
    full = lambda a: pl.BlockSpec(a.shape, lambda b, c: (0,) * a.ndim)
    return pl.pallas_call(
        _ret_kernel,
        grid=(batch, nc),
        in_specs=[row, row, row, full(intra), full(qdec), full(kdec), full(sdec), full(gn)],
        out_specs=[row, pl.BlockSpec((None, N_HEADS, HEAD_DIM, HEAD_DIM), lambda b, c: (b, 0, 0, 0))],
        out_shape=[jax.ShapeDtypeStruct((batch * seq, RET_W), BF16),
                   jax.ShapeDtypeStruct((batch, N_HEADS, HEAD_DIM, HEAD_DIM), F32)],
        scratch_shapes=[pltpu.VMEM((N_HEADS, HEAD_DIM, HEAD_DIM), F32)],
        compiler_params=_cparams("parallel", "arbitrary"),
        name="ret_prompt",
    )(rq, rk, rv, intra, qdec, kdec, sdec, gn)


def _mla_kernel(qa_ref, kv_ref, vt_ref, wuv_ref, mla_o, q_scr, m_scr, l_scr, acc_scr, *, tq):
    i = pl.program_id(1)
    cols = N_HEADS * tq
    c2 = MLA_SCALE * LOG2E
    for h in range(N_HEADS):
        q_scr[h * tq:(h + 1) * tq, :] = qa_ref[:, h * QK_W:(h + 1) * QK_W]
    m_scr[...] = jnp.full_like(m_scr, NEG_BIG)
    l_scr[...] = jnp.zeros_like(l_scr)
    acc_scr[...] = jnp.zeros_like(acc_scr)

    def block(blk, n_sub, masked):
        nk = n_sub * tq
        kj = kv_ref[pl.ds(pl.multiple_of(blk * tq, tq), nk), :]
        s = _dot_nt(kj, q_scr[...])
        if masked:
            kpos = blk * tq + lax.broadcasted_iota(jnp.int32, (nk, cols), 0)
            qpos = i * tq + jnp.bitwise_and(lax.broadcasted_iota(jnp.int32, (nk, cols), 1), tq - 1)
            s = jnp.where(kpos <= qpos, s, NEG_BIG)
        m_old = m_scr[...]
        m_new = jnp.maximum(m_old, jnp.max(s, axis=0, keepdims=True))
        alpha = jnp.exp2((m_old - m_new) * c2)
        p = jnp.exp2((s - m_new) * c2)
        l_scr[...] = alpha * l_scr[...] + jnp.sum(p, axis=0, keepdims=True)
        pb = p.astype(BF16)
        pv = _dot(vt_ref[blk], pb[0:tq, :])
        for u in range(1, n_sub):
            pv = pv + _dot(vt_ref[blk + u], pb[u * tq:(u + 1) * tq, :])
        acc_scr[...] = alpha * acc_scr[...] + pv
        m_scr[...] = m_new

    def body(j, carry):
        block(2 * j, 2, False)
        return carry

    lax.fori_loop(0, i // 2, body, 0)

    @pl.when(i % 2 == 1)
    def _():
        block(i - 1, 1, False)

    block(i, 1, True)
    lat_t = acc_scr[...] / l_scr[...]
    for h in range(N_HEADS):
        lat = jnp.transpose(lat_t[:, h * tq:(h + 1) * tq]).astype(BF16)
        mla_o[:, h * MLA_V:(h + 1) * MLA_V] = _dot(lat, wuv_ref[h]).astype(BF16)


def _mla_prompt(qa, kv, vt, wuv, batch, seq, tq):
    nq = seq // tq
    assert vt.shape == (batch, nq, KV_LORA, tq)
    return pl.pallas_call(
        functools.partial(_mla_kernel, tq=tq),
        grid=(batch, nq),
        in_specs=[pl.BlockSpec((tq, N_HEADS * QK_W), lambda b, i: (b * nq + i, 0)),
                  pl.BlockSpec((seq, QK_W), lambda b, i: (b, 0)),
                  pl.BlockSpec((None, nq, KV_LORA, tq), lambda b, i: (b, 0, 0, 0)),
                  pl.BlockSpec(wuv.shape, lambda b, i: (0, 0, 0))],
        out_specs=pl.BlockSpec((tq, N_HEADS * MLA_V), lambda b, i: (b * nq + i, 0)),
        out_shape=jax.ShapeDtypeStruct((batch * seq, N_HEADS * MLA_V), BF16),
        scratch_shapes=[pltpu.VMEM((N_HEADS * tq, QK_W), BF16), pltpu.VMEM((1, N_HEADS * tq), F32),
                        pltpu.VMEM((1, N_HEADS * tq), F32), pltpu.VMEM((KV_LORA, N_HEADS * tq), F32)],
        compiler_params=_cparams("parallel", "arbitrary"),
        name="mla_prompt",
    )(qa, kv, vt, wuv)


def _memkv_kernel(mem_ref, g_ref, w_ref, k_o, v_o):
    kvp = _dot(_rms(mem_ref[...], g_ref[...]).astype(BF16), w_ref[...])
    k_o[...] = kvp[:, 0:MEM_W]
    v_o[...] = kvp[:, MEM_W:2 * MEM_W]


def _mem_kv(mem2, g, w, tm):
    n, d = mem2.shape
    row = lambda wd: pl.BlockSpec((tm, wd), lambda i: (i, 0))
    full = lambda a: pl.BlockSpec(a.shape, lambda i: (0,) * a.ndim)
    return pl.pallas_call(
        _memkv_kernel,
        grid=(n // tm,),
        in_specs=[row(d), full(g), full(w)],
        out_specs=[row(MEM_W), row(MEM_W)],
        out_shape=[jax.ShapeDtypeStruct((n, MEM_W), F32)] * 2,
        compiler_params=_cparams("parallel"),
        name="mem_kv",
    )(mem2, g, w)


def _out_proj(x, cat_scr, wout_ref, gf_ref):
    return _rms(x + _dot(cat_scr[...], wout_ref[...]), gf_ref[...])


def _merge_prompt_kernel(x_ref, ret_ref, mla_ref, mq_ref, sg_ref, mk_ref, mv_ref, wout_ref, gf_ref,
                         y_o, cat_scr):
    cat_scr[:, 0:RET_W] = ret_ref[...] * sg_ref[:, 0:RET_W]
    cat_scr[:, RET_W:2 * RET_W] = mla_ref[...] * sg_ref[:, RET_W:2 * RET_W]
    for h in range(N_HEADS):
        sl = slice(h * HEAD_DIM, (h + 1) * HEAD_DIM)
        s = _dot_nt(mq_ref[:, sl], mk_ref[:, sl].astype(BF16)) * MEM_SCALE
        p = jnp.exp(s - jnp.max(s, axis=-1, keepdims=True))
        p = (p / jnp.sum(p, axis=-1, keepdims=True)).astype(BF16)
        o = _dot(p, mv_ref[:, sl].astype(BF16))
        gsl = slice(2 * RET_W + h * HEAD_DIM, 2 * RET_W + (h + 1) * HEAD_DIM)
        cat_scr[:, gsl] = (o * sg_ref[:, gsl].astype(F32)).astype(BF16)
    y_o[...] = _out_proj(x_ref[...], cat_scr, wout_ref, gf_ref)


def _merge_prompt(x2, ret_n, mla, mq, sg, mk, mv, wout, gf, batch, seq, tm):
    nt = seq // tm
    n_mem = mk.shape[0] // batch
    row = lambda w: pl.BlockSpec((tm, w), lambda b, t: (b * nt + t, 0))
    mem = pl.BlockSpec((n_mem, MEM_W), lambda b, t: (b, 0))
    full = lambda a: pl.BlockSpec(a.shape, lambda b, t: (0,) * a.ndim)
    return pl.pallas_call(
        _merge_prompt_kernel,
        grid=(batch, nt),
        in_specs=[row(x2.shape[1]), row(RET_W), row(RET_W), row(MEM_W), row(D_MIX), mem, mem, full(wout), full(gf)],
        out_specs=row(x2.shape[1]),
        out_shape=jax.ShapeDtypeStruct(x2.shape, F32),
        scratch_shapes=[pltpu.VMEM((tm, D_MIX), BF16)],
        compiler_params=_cparams("parallel", "arbitrary"),
        name="merge_prompt",
    )(x2, ret_n, mla, mq, sg, mk, mv, wout, gf)


def _merge_sample_kernel(x_ref, ret_ref, lat_ref, memo_ref, sg_ref, wuv_ref, wout_ref, gf_ref, y_o, cat_scr):
    cat_scr[:, 0:RET_W] = ret_ref[...] * sg_ref[:, 0:RET_W]
    for h in range(N_HEADS):
        lat = lat_ref[:, h * KV_LORA:(h + 1) * KV_LORA].astype(BF16)
        gsl = slice(RET_W + h * MLA_V, RET_W + (h + 1) * MLA_V)
        cat_scr[:, gsl] = (_dot(lat, wuv_ref[h]) * sg_ref[:, gsl].astype(F32)).astype(BF16)
    cat_scr[:, 2 * RET_W:D_MIX] = (memo_ref[...] * sg_ref[:, 2 * RET_W:D_MIX].astype(F32)).astype(BF16)
    y_o[...] = _out_proj(x_ref[...], cat_scr, wout_ref, gf_ref)


def _merge_sample(x2, ret_n, lat, memo, sg, wuv, wout, gf):
    args = (x2, ret_n, lat, memo, sg, wuv, wout, gf)
    return pl.pallas_call(
        _merge_sample_kernel,
        grid=(1,),
        in_specs=[pl.BlockSpec(a.shape, lambda i, nd=a.ndim: (0,) * nd) for a in args],
        out_specs=pl.BlockSpec(x2.shape, lambda i: (0, 0)),
        out_shape=jax.ShapeDtypeStruct(x2.shape, F32),
        scratch_shapes=[pltpu.VMEM((x2.shape[0], D_MIX), BF16)],
        compiler_params=_cparams("arbitrary"),
        name="merge_sample",
    )(*args)


def _ret_step_kernel(rq_ref, rk_ref, rv_ref, s_ref, gam_ref, gn_ref, ret_o, s_o, *, bt):
    sq = (HEAD_DIM, HEAD_DIM)
    for b in range(bt):
        for h in range(N_HEADS):
            sl = slice(h * HEAD_DIM, (h + 1) * HEAD_DIM)
            q = rq_ref[b:b + 1, sl].astype(F32)
            k = rk_ref[b:b + 1, sl].astype(F32)
            v = rv_ref[b:b + 1, sl].astype(F32)
            gam = gam_ref[h]
            s_old = s_ref[b, h]
            q_col = jnp.transpose(jnp.broadcast_to(q, sq))
            k_col = jnp.transpose(jnp.broadcast_to(k, sq))
            qk = jnp.sum(q * k, axis=-1, keepdims=True)
            o = qk * v + gam * jnp.sum(q_col * s_old, axis=0, keepdims=True)
            s_o[b, h] = s_old * gam + k_col * v
            mu = jnp.mean(o, axis=-1, keepdims=True)
            d = o - mu
            var = jnp.mean(d * d, axis=-1, keepdims=True)
            ret_o[b:b + 1, sl] = (d * lax.rsqrt(var + EPS) * gn_ref[:, sl]).astype(BF16)


def _retention_step(rq, rk, rv, state, gn, bt):
    n = rq.shape[0]
    log_g = jnp.log1p(-jnp.exp2(-5.0 - jnp.arange(N_HEADS, dtype=F32)))
    gam = jnp.broadcast_to(jnp.exp(log_g)[:, None, None], (N_HEADS, 1, HEAD_DIM))
    row = pl.BlockSpec((bt, RET_W), lambda i: (i, 0))
    st = pl.BlockSpec((bt, N_HEADS, HEAD_DIM, HEAD_DIM), lambda i: (i, 0, 0, 0))
    full = lambda a: pl.BlockSpec(a.shape, lambda i: (0,) * a.ndim)
    return pl.pallas_call(
        functools.partial(_ret_step_kernel, bt=bt),
        grid=(n // bt,),
        in_specs=[row, row, row, st, full(gam), full(gn)],
        out_specs=[row, st],
        out_shape=[jax.ShapeDtypeStruct((n, RET_W), BF16), jax.ShapeDtypeStruct(state.shape, F32)],
        compiler_params=_cparams("parallel"),
        name="ret_step",
    )(rq, rk, rv, state, gam, gn)


def _mla_dec_kernel(pt_ref, q_ref, kvn_ref, ckv_hbm, kpet_hbm, o_ref,
                    ckv_buf, kpe_buf, sems, m_scr, l_scr, acc_scr, *, cp, page, n_sub):
    b, j = pl.program_id(0), pl.program_id(1)
    nb, nj = pl.num_programs(0), pl.num_programs(1)
    t = b * nj + j
    slot = t % 2
    last = t == nb * nj - 1
    wrap = j + 1 == nj
    b_next = jnp.where(last, b, jnp.where(wrap, b + 1, b))
    j_next = jnp.where(last, j, jnp.where(wrap, 0, j + 1))

    def page_copies(bb, jj, sl, r, lookup=True):
        pg = pt_ref[bb, jj * cp + r] if lookup else 0
        return (pltpu.make_async_copy(ckv_hbm.at[pg], ckv_buf.at[sl, r], sems.at[0, sl]),
                pltpu.make_async_copy(kpet_hbm.at[pg], kpe_buf.at[sl, :, r * page:(r + 1) * page], sems.at[1, sl]))

    @pl.when(t == 0)
    def _():
        for r in range(cp):
            for cpy in page_copies(b, j, slot, r):
                cpy.start()

    @pl.when(j == 0)
    def _():
        m_scr[...] = jnp.full_like(m_scr, NEG_BIG)
        l_scr[...] = jnp.zeros_like(l_scr)
        acc_scr[...] = jnp.zeros_like(acc_scr)

    q = q_ref[...]
    ql, qp = q[:, 0:KV_LORA], q[:, KV_LORA:KV_LORA + MLA_ROPE]
    c2 = MLA_SCALE * LOG2E

    def update(s_parts, v_loaders, between=None):
        m_old = m_scr[...]
        m_new = m_old
        for s in s_parts:
            m_new = jnp.maximum(m_new, jnp.max(s, axis=-1, keepdims=True))
        alpha = jnp.exp2((m_old - m_new) * c2)
        l = alpha * l_scr[...]
        acc = alpha * acc_scr[...]
        for u, (s, load_v) in enumerate(zip(s_parts, v_loaders)):
            p = jnp.exp2((s - m_new) * c2)
            l = l + jnp.sum(p, axis=-1, keepdims=True)
            acc = acc + _dot(p.astype(BF16), load_v())
            if between is not None:
                between(u)
        m_scr[...] = m_new
        l_scr[...] = l
        acc_scr[...] = acc

    for r in range(cp):
        for cpy in page_copies(b, j, slot, r, lookup=False):
            cpy.wait()
    per = cp // n_sub

    def load_keys(u):
        return ckv_buf[slot, u * per:(u + 1) * per].reshape(per * page, KV_LORA).astype(BF16)

    def start_next(u, which):
        for r in range(u * per, (u + 1) * per):
            page_copies(b_next, j_next, 1 - slot, r)[which].start()

    s_parts = []
    for u in range(n_sub):
        ks = slice(u * per * page, (u + 1) * per * page)
        s_parts.append(_dot_nt(ql, load_keys(u)) + _dot(qp, kpe_buf[slot, :, ks].astype(BF16)))
        start_next(u, 0)
    update(s_parts, [functools.partial(load_keys, u) for u in range(n_sub)],
           between=lambda u: start_next(u, 1))

    @pl.when(j == nj - 1)
    def _():
        kvn = kvn_ref[...].astype(BF16)
        s = _dot_nt(ql, kvn[:, 0:KV_LORA]) + _dot_nt(qp, kvn[:, KV_LORA:KV_LORA + MLA_ROPE])
        s = jnp.where(lax.broadcasted_iota(jnp.int32, s.shape, 1) == 0, s, NEG_BIG)
        update([s], [lambda: kvn[:, 0:KV_LORA]])
        o_ref[...] = acc_scr[...] / l_scr[...]

    @pl.when(last)
    def _():
        for r in range(cp):
            for cpy in page_copies(b, j, 1 - slot, r, lookup=False):
                cpy.wait()


def _mla_decode(q8, kvn8, pool_ckv, pool_kpet, page_table, cp):
    n, n_pages = page_table.shape
    page = pool_ckv.shape[1]
    qw = q8.shape[-1]
    spec_q = pl.BlockSpec((None, 8, qw), lambda b, j, pt: (b, 0, 0))
    grid_spec = pltpu.PrefetchScalarGridSpec(
        num_scalar_prefetch=1,
        grid=(n, n_pages // cp),
        in_specs=[spec_q, spec_q, pl.BlockSpec(memory_space=pl.ANY), pl.BlockSpec(memory_space=pl.ANY)],
        out_specs=pl.BlockSpec((None, 8, KV_LORA), lambda b, j, pt: (b, 0, 0)),
        scratch_shapes=[pltpu.VMEM((2, cp, page, KV_LORA), F32), pltpu.VMEM((2, MLA_ROPE, cp * page), F32),
                        pltpu.SemaphoreType.DMA((2, 2)),
                        pltpu.VMEM((8, 1), F32), pltpu.VMEM((8, 1), F32), pltpu.VMEM((8, KV_LORA), F32)],
    )
    return pl.pallas_call(
        functools.partial(_mla_dec_kernel, cp=cp, page=page, n_sub=1),
        grid_spec=grid_spec,
        out_shape=jax.ShapeDtypeStruct((n, 8, KV_LORA), F32),
        compiler_params=_cparams("arbitrary", "arbitrary"),
        name="mla_decode",
    )(page_table, q8, kvn8, pool_ckv, pool_kpet)


def _mem_dec_kernel(q_ref, mk_ref, mv_ref, o_ref, *, bt):
    n_col = mk_ref.shape[1]
    col_head = jnp.bitwise_and(lax.broadcasted_iota(jnp.int32, (8, n_col), 1), N_HEADS - 1)
    own = col_head == lax.broadcasted_iota(jnp.int32, (8, n_col), 0)
    pad = jnp.zeros((8 - N_HEADS, HEAD_DIM), BF16)
    for b in range(bt):
        q8 = jnp.concatenate([q_ref[b], pad], axis=0)
        s = jnp.where(own, _dot_nt(q8, mk_ref[b].astype(BF16)) * MEM_SCALE, NEG_BIG)
        p = jnp.exp(s - jnp.max(s, axis=-1, keepdims=True))
        p = (p / jnp.sum(p, axis=-1, keepdims=True)).astype(BF16)
        o_ref[b] = _dot(p, mv_ref[b].astype(BF16))[0:N_HEADS, :]


def _mem_decode(mq, mk, mv, bt):
    n, n_col, _ = mk.shape
    row = pl.BlockSpec((bt, N_HEADS, HEAD_DIM), lambda i: (i, 0, 0))
    mem = pl.BlockSpec((bt, n_col, HEAD_DIM), lambda i: (i, 0, 0))
    return pl.pallas_call(
        functools.partial(_mem_dec_kernel, bt=bt),
        grid=(n // bt,),
        in_specs=[row, mem, mem],
        out_specs=row,
        out_shape=jax.ShapeDtypeStruct((n, N_HEADS, HEAD_DIM), F32),
        compiler_params=_cparams("parallel"),
        name="mem_decode",
    )(mq, mk, mv)


def _rope_tables(pos, n_freq, slot):
    inv = ROPE_BASE ** (-jnp.arange(0, 2 * n_freq, 2, dtype=F32) / (2 * n_freq))
    ang = pos.astype(F32)[:, None] * inv[None, :]
    c, s = jnp.cos(ang), jnp.sin(ang)
    z = jnp.zeros((pos.shape[0], slot // 2 - n_freq), F32)
    return jnp.concatenate([c, z, c, z], axis=1), jnp.concatenate([-s, z, s, z], axis=1)


def _spread(w):
    z = jnp.zeros(w.shape[:-1] + (32,), w.dtype)
    return jnp.concatenate([w[..., 0:32], z, w[..., 32:64], z], axis=-1)


def _prep_weights(w_in, w_uq, w_uk, w_uv, w_mem_kv, w_out):
    o = [0, 512, 1024, 1536, 1920, 2176, 2240, 2752, 4288]
    seg = lambda i: w_in[:, o[i]:o[i + 1]]
    win = jnp.concatenate([seg(0), seg(1), seg(2), seg(3), seg(4), seg(6), seg(7), _spread(seg(5))], axis=1)
    wuq = jnp.concatenate([w_uq[:, :, :MLA_NOPE].reshape(Q_LORA, -1),
                           _spread(w_uq[:, :, MLA_NOPE:]).reshape(Q_LORA, -1)], axis=1)
    wuk = jnp.transpose(w_uk, (1, 2, 0))
    wuv = jnp.transpose(w_uv, (1, 0, 2))
    return tuple(a.astype(BF16) for a in (win, wuq, wuk, wuv, w_mem_kv, w_out))


def kernel(x_prompt, x_sample, mem_prompt, cache_ckv, cache_kpe, page_table, state_ret, cache_mem_k, cache_mem_v,
           norm_g, w_in, ret_gn_g, mla_qnorm_g, w_uq, w_uk, mla_kvnorm_g, w_uv, mem_norm_g, w_mem_kv, w_out,
           final_norm_g):
    batch, seq, d_model = x_prompt.shape
    n_dec = x_sample.shape[0]
    n_mem = mem_prompt.shape[1]
    depth = w_in.shape[0]
    assert depth == 1 and x_sample.shape[1] == 1
    l = 0
    win, wuq, wuk, wuv, wmem, wout = _prep_weights(w_in[l], w_uq[l], w_uk[l], w_uv[l], w_mem_kv[l], w_out[l])
    g_in = norm_g[l][None, :]
    g_q = mla_qnorm_g[l][None, :]
    g_kv = mla_kvnorm_g[l][None, :]
    g_gn = ret_gn_g[l][None, :]
    g_mem = mem_norm_g[l][None, :]
    g_fin = final_norm_g[None, :]

    tm = min(512, seq)
    tq = min(256, seq)
    n_pages = page_table.shape[1]
    cp = min(64, n_pages)

    xp = x_prompt.reshape(batch * seq, d_model)
    pos_p = jnp.arange(seq, dtype=jnp.int32)
    tabs_p = _rope_tables(pos_p, HEAD_DIM // 2, LANES) + _rope_tables(pos_p, MLA_ROPE // 2, LANES)
    rq, rk, rv, qa, kvb, ckv, mq, sg, kpet, vt = _project(
        xp, tabs_p, g_in, win, g_q, wuq, wuk, g_kv, batch, seq, tm, tq)
    ret_n, ret_state_p = _retention_prompt(rq, rk, rv, g_gn, batch, seq, tm)
    mla = _mla_prompt(qa, kvb, vt, wuv, batch, seq, tq)
    mk, mv = _mem_kv(mem_prompt.reshape(batch * n_mem, d_model), g_mem, wmem, min(512, batch * n_mem))
    y_p = _merge_prompt(xp, ret_n, mla, mq, sg, mk, mv, wout, g_fin, batch, seq, tm)

    xs = x_sample.reshape(n_dec, d_model)
    pos_s = jnp.full((n_dec,), PAST_LEN, dtype=jnp.int32)
    tabs_s = _rope_tables(pos_s, HEAD_DIM // 2, LANES) + _rope_tables(pos_s, MLA_ROPE // 2, LANES)
    rq_s, rk_s, rv_s, qa_s, _, ckv_s, mq_s, sg_s, kpet_s, _ = _project(
        xs, tabs_s, g_in, win, g_q, wuq, wuk, g_kv, 1, n_dec, n_dec, n_dec)
    kpe_s = jnp.swapaxes(kpet_s[0], 0, 1)
    ret_n_s, ret_state_s = _retention_step(rq_s, rk_s, rv_s, state_ret[l], g_gn, 8)
    qh = qa_s.reshape(n_dec, N_HEADS, QK_W)
    q_std = jnp.concatenate([qh[..., 0:KV_LORA], qh[..., KV_LORA:KV_LORA + 32], qh[..., KV_LORA + 64:KV_LORA + 96]], -1)
    q8 = jnp.pad(q_std, ((0, 0), (0, 8 - N_HEADS), (0, 0)))
    kvn8 = jnp.pad(jnp.concatenate([ckv_s, kpe_s], axis=-1)[:, None, :], ((0, 0), (0, 7), (0, 0)))
    lat_s = _mla_decode(q8, kvn8, cache_ckv[l], jnp.swapaxes(cache_kpe[l], 1, 2), page_table, cp)
    lat_s = lat_s[:, 0:N_HEADS, :].reshape(n_dec, N_HEADS * KV_LORA)
    memo_s = _mem_decode(mq_s.reshape(n_dec, N_HEADS, HEAD_DIM),
                         cache_mem_k[l].reshape(n_dec, n_mem * N_HEADS, HEAD_DIM),
                         cache_mem_v[l].reshape(n_dec, n_mem * N_HEADS, HEAD_DIM), 8).reshape(n_dec, MEM_W)
    y_s = _merge_sample(xs, ret_n_s, lat_s, memo_s, sg_s, wuv, wout, g_fin)

    return (y_p.reshape(batch, seq, d_model), y_s.reshape(n_dec, 1, d_model),
            ckv.reshape(1, batch, seq, KV_LORA), jnp.swapaxes(kpet, 1, 2)[None],
            ret_state_p[None], mk.reshape(1, batch, n_mem, N_HEADS, HEAD_DIM), mv.reshape(1, batch, n_mem, N_HEADS, HEAD_DIM),
            ckv_s.reshape(1, n_dec, 1, KV_LORA), kpe_s.reshape(1, n_dec, 1, MLA_ROPE), ret_state_s[None])
```

```python
import functools

import jax
import jax.numpy as jnp
from jax import lax
from jax.experimental import pallas as pl
from jax.experimental.pallas import tpu as pltpu

F32 = jnp.float32
BF16 = jnp.bfloat16

HEAD_DIM = 128
N_HEADS = 4
RET_W = N_HEADS * HEAD_DIM
MLA_NOPE = 128
MLA_ROPE = 64
MLA_V = 128
Q_LORA = 384
KV_LORA = 256
MEM_W = N_HEADS * HEAD_DIM
D_MIX = 3 * RET_W
RET_CHUNK = 256
PAST_LEN = 16384
ROPE_BASE = 10000.0
EPS = 1e-6
MLA_SCALE = (MLA_NOPE + MLA_ROPE) ** -0.5
MEM_SCALE = HEAD_DIM ** -0.5
RK_SCALE = HEAD_DIM ** -0.5
NEG_BIG = -1e30
LOG2E = 1.4426950408889634
N_SLOTS = 3

LANES = 128
QK_W = KV_LORA + LANES
OFF_RQ, OFF_RK, OFF_RV = 0, RET_W, 2 * RET_W
OFF_CQ = 3 * RET_W
OFF_CKV = OFF_CQ + Q_LORA
OFF_MQ = OFF_CKV + KV_LORA
OFF_GATE = OFF_MQ + MEM_W
OFF_KPE = OFF_GATE + D_MIX
D_IN2 = OFF_KPE + LANES

VMEM_LIMIT = 48 * 1024 * 1024


def _cparams(*sem):
    return pltpu.CompilerParams(dimension_semantics=sem, vmem_limit_bytes=VMEM_LIMIT)


def _rms(x, g):
    return x * lax.rsqrt(jnp.mean(x * x, axis=-1, keepdims=True) + EPS) * g


def _dot(a, b):
    return jnp.dot(a, b, preferred_element_type=F32)


def _dot_nt(a, b):
    return lax.dot_general(a, b, (((1,), (1,)), ((), ())), preferred_element_type=F32)


def _rot_half(x, cos, sin):
    return x * cos + pltpu.roll(x, 64, 1) * sin


def _proj_kernel(x_ref, cr_ref, sr_ref, cm_ref, sm_ref, g_ref, win_ref, gq_ref, wuq_ref, wuk_ref, gkv_ref,
                 rq_o, rk_o, rv_o, qa_o, kv_o, ckv_o, mq_o, sg_o, kpet_o, vt_o):
    xn = _rms(x_ref[...], g_ref[...]).astype(BF16)
    cr, sr = cr_ref[...], sr_ref[...]
    cm, sm = cm_ref[...], sm_ref[...]

    zq = _dot(xn, win_ref[:, OFF_RQ:OFF_RQ + RET_W])
    zk = _dot(xn, win_ref[:, OFF_RK:OFF_RK + RET_W])
    for h in range(N_HEADS):
        sl = slice(h * HEAD_DIM, (h + 1) * HEAD_DIM)
        rq_o[:, sl] = _rot_half(zq[:, sl], cr, sr).astype(BF16)
        rk_o[:, sl] = (_rot_half(zk[:, sl], cr, sr) * RK_SCALE).astype(BF16)
    rv_o[...] = _dot(xn, win_ref[:, OFF_RV:OFF_RV + RET_W]).astype(BF16)

    cq = _rms(_dot(xn, win_ref[:, OFF_CQ:OFF_CQ + Q_LORA]), gq_ref[...]).astype(BF16)
    q = _dot(cq, wuq_ref[...])
    for h in range(N_HEADS):
        qn = q[:, h * MLA_NOPE:(h + 1) * MLA_NOPE].astype(BF16)
        qa_o[:, h * QK_W:h * QK_W + KV_LORA] = _dot(qn, wuk_ref[h]).astype(BF16)
        qp = q[:, RET_W + h * LANES:RET_W + (h + 1) * LANES]
        qa_o[:, h * QK_W + KV_LORA:(h + 1) * QK_W] = _rot_half(qp, cm, sm).astype(BF16)

    ckv = _rms(_dot(xn, win_ref[:, OFF_CKV:OFF_CKV + KV_LORA]), gkv_ref[...])
    ckv_o[...] = ckv
    kv_o[:, 0:KV_LORA] = ckv.astype(BF16)
    kp = _rot_half(_dot(xn, win_ref[:, OFF_KPE:OFF_KPE + LANES]), cm, sm)
    kv_o[:, KV_LORA:QK_W] = kp.astype(BF16)
    kpt = jnp.transpose(kp)
    kpet_o[0:32, :] = kpt[0:32, :]
    kpet_o[32:64, :] = kpt[64:96, :]
    vt = jnp.transpose(ckv).astype(BF16)
    tkb = vt_o.shape[-1]
    for u in range(vt_o.shape[0]):
        vt_o[u] = vt[:, u * tkb:(u + 1) * tkb]

    mq_o[...] = _dot(xn, win_ref[:, OFF_MQ:OFF_MQ + MEM_W]).astype(BF16)
    gate = _dot(xn, win_ref[:, OFF_GATE:OFF_GATE + D_MIX])
    sg_o[...] = (gate / (1.0 + jnp.exp(-gate))).astype(BF16)


def _project(x2, tabs, g, win, gq, wuq, wuk, gkv, batch, seq, tm, tkb):
    n, d = x2.shape
    nt = seq // tm
    row = lambda w: pl.BlockSpec((tm, w), lambda i: (i, 0))
    tab = pl.BlockSpec((tm, LANES), lambda i: (i % nt, 0))
    full = lambda a: pl.BlockSpec(a.shape, lambda i: (0,) * a.ndim)
    outs = [(RET_W, BF16), (RET_W, BF16), (RET_W, BF16), (N_HEADS * QK_W, BF16), (QK_W, BF16),
            (KV_LORA, F32), (MEM_W, BF16), (D_MIX, BF16)]
    kpet_spec = pl.BlockSpec((None, MLA_ROPE, tm), lambda i: (i // nt, 0, i % nt))
    vt_spec = pl.BlockSpec((None, tm // tkb, KV_LORA, tkb), lambda i: (i // nt, i % nt, 0, 0))
    return pl.pallas_call(
        _proj_kernel,
        grid=(n // tm,),
        in_specs=[row(d), tab, tab, tab, tab, full(g), full(win), full(gq), full(wuq), full(wuk), full(gkv)],
        out_specs=[row(w) for w, _ in outs] + [kpet_spec, vt_spec],
        out_shape=[jax.ShapeDtypeStruct((n, w), dt) for w, dt in outs]
                  + [jax.ShapeDtypeStruct((batch, MLA_ROPE, seq), F32),
                     jax.ShapeDtypeStruct((batch, seq // tkb, KV_LORA, tkb), BF16)],
        compiler_params=_cparams("parallel"),
        name="proj",
    )(x2, *tabs, g, win, gq, wuq, wuk, gkv)


def _ret_kernel(rq_ref, rk_ref, rv_ref, intra_ref, qdec_ref, kdec_ref, sdec_ref, gn_ref,
                ret_o, state_o, s_scr):
    c = pl.program_id(1)

    @pl.when(c == 0)
    def _():
        s_scr[...] = jnp.zeros_like(s_scr)

    chunk = intra_ref.shape[1]
    for ci in range(rq_ref.shape[0] // chunk):
        rs = slice(ci * chunk, (ci + 1) * chunk)
        for h in range(N_HEADS):
            sl = slice(h * HEAD_DIM, (h + 1) * HEAD_DIM)
            q, k, v = rq_ref[rs, sl], rk_ref[rs, sl], rv_ref[rs, sl]
            s_old = s_scr[h]
            sc = _dot_nt(q, k) * intra_ref[h]
            o = _dot(sc.astype(BF16), v) + qdec_ref[h] * _dot(q, s_old.astype(BF16))
            kd_t = jnp.transpose(k.astype(F32) * kdec_ref[h]).astype(BF16)
            s_scr[h] = s_old * sdec_ref[h] + _dot(kd_t, v)
            mu = jnp.mean(o, axis=-1, keepdims=True)
            d = o - mu
            var = jnp.mean(d * d, axis=-1, keepdims=True)
            ret_o[rs, sl] = (d * lax.rsqrt(var + EPS) * gn_ref[:, sl]).astype(BF16)

    @pl.when(c == pl.num_programs(1) - 1)
    def _():
        state_o[...] = s_scr[...]


def _ret_tables(chunk):
    log_g = jnp.log1p(-jnp.exp2(-5.0 - jnp.arange(N_HEADS, dtype=F32)))
    idx = jnp.arange(chunk, dtype=F32)
    diff = idx[:, None] - idx[None, :]
    intra = jnp.where(diff[None] >= 0, jnp.exp(jnp.maximum(diff, 0.0)[None] * log_g[:, None, None]), 0.0)
    q_dec = jnp.exp((idx[None, :] + 1.0) * log_g[:, None])
    k_dec = jnp.exp((chunk - 1.0 - idx)[None, :] * log_g[:, None])
    s_dec = jnp.exp(chunk * log_g)
    bc = lambda a: jnp.broadcast_to(a[:, :, None], (N_HEADS, chunk, HEAD_DIM))
    return intra, bc(q_dec), bc(k_dec), jnp.broadcast_to(s_dec[:, None, None], (N_HEADS, 1, HEAD_DIM))


def _retention_prompt(rq, rk, rv, gn, batch, seq, tr):
    chunk = RET_CHUNK
    assert seq % tr == 0 and tr % chunk == 0
    nc = seq // tr
    intra, qdec, kdec, sdec = _ret_tables(chunk)
    row = pl.BlockSpec((tr, RET_W), lambda b, c: (b * nc + c, 0))
    full = lambda a: pl.BlockSpec(a.shape, lambda b, c: (0,) * a.ndim)
    return pl.pallas_call(
        _ret_kernel,
        grid=(batch, nc),
        in_specs=[row, row, row, full(intra), full(qdec), full(kdec), full(sdec), full(gn)],
        out_specs=[row, pl.BlockSpec((None, N_HEADS, HEAD_DIM, HEAD_DIM), lambda b, c: (b, 0, 0, 0))],
        out_shape=[jax.ShapeDtypeStruct((batch * seq, RET_W), BF16),
                   jax.ShapeDtypeStruct((batch, N_HEADS, HEAD_DIM, HEAD_DIM), F32)],
        scratch_shapes=[pltpu.VMEM((N_HEADS, HEAD_DIM, HEAD_DIM), F32)],
        compiler_params=_cparams("parallel", "arbitrary"),
        name="ret_prompt",
    )(rq, rk, rv, intra, qdec, kdec, sdec, gn)


def _mla_kernel(qa_ref, kv_ref, vt_ref, wuv_ref, mla_o, q_scr, m_scr, l_scr, acc_scr, *, tq):
    i = pl.program_id(1)
    cols = N_HEADS * tq
    c2 = MLA_SCALE * LOG2E
    for h in range(N_HEADS):
        q_scr[h * tq:(h + 1) * tq, :] = qa_ref[:, h * QK_W:(h + 1) * QK_W]
    m_scr[...] = jnp.full_like(m_scr, NEG_BIG)
    l_scr[...] = jnp.zeros_like(l_scr)
    acc_scr[...] = jnp.zeros_like(acc_scr)

    def block(blk, n_sub, masked):
        nk = n_sub * tq
        kj = kv_ref[pl.ds(pl.multiple_of(blk * tq, tq), nk), :]
        s = _dot_nt(kj, q_scr[...])
        if masked:
            kpos = blk * tq + lax.broadcasted_iota(jnp.int32, (nk, cols), 0)
            qpos = i * tq + jnp.bitwise_and(lax.broadcasted_iota(jnp.int32, (nk, cols), 1), tq - 1)
            s = jnp.where(kpos <= qpos, s, NEG_BIG)
        m_old = m_scr[...]
        m_new = jnp.maximum(m_old, jnp.max(s, axis=0, keepdims=True))
        alpha = jnp.exp2((m_old - m_new) * c2)
        p = jnp.exp2((s - m_new) * c2)
        l_scr[...] = alpha * l_scr[...] + jnp.sum(p, axis=0, keepdims=True)
        pb = p.astype(BF16)
        pv = _dot(vt_ref[blk], pb[0:tq, :])
        for u in range(1, n_sub):
            pv = pv + _dot(vt_ref[blk + u], pb[u * tq:(u + 1) * tq, :])
        acc_scr[...] = alpha * acc_scr[...] + pv
        m_scr[...] = m_new

    def body(j, carry):
        block(2 * j, 2, False)
        return carry

    lax.fori_loop(0, i // 2, body, 0)

    @pl.when(i % 2 == 1)
    def _():
        block(i - 1, 1, False)

    block(i, 1, True)
    lat_t = acc_scr[...] / l_scr[...]
    for h in range(N_HEADS):
        lat = jnp.transpose(lat_t[:, h * tq:(h + 1) * tq]).astype(BF16)
        mla_o[:, h * MLA_V:(h + 1) * MLA_V] = _dot(lat, wuv_ref[h]).astype(BF16)


def _mla_prompt(qa, kv, vt, wuv, batch, seq, tq):
    nq = seq // tq
    assert vt.shape == (batch, nq, KV_LORA, tq)
    return pl.pallas_call(
        functools.partial(_mla_kernel, tq=tq),
        grid=(batch, nq),
        in_specs=[pl.BlockSpec((tq, N_HEADS * QK_W), lambda b, i: (b * nq + i, 0)),
                  pl.BlockSpec((seq, QK_W), lambda b, i: (b, 0)),
                  pl.BlockSpec((None, nq, KV_LORA, tq), lambda b, i: (b, 0, 0, 0)),
                  pl.BlockSpec(wuv.shape, lambda b, i: (0, 0, 0))],
        out_specs=pl.BlockSpec((tq, N_HEADS * MLA_V), lambda b, i: (b * nq + i, 0)),
        out_shape=jax.ShapeDtypeStruct((batch * seq, N_HEADS * MLA_V), BF16),
        scratch_shapes=[pltpu.VMEM((N_HEADS * tq, QK_W), BF16), pltpu.VMEM((1, N_HEADS * tq), F32),
                        pltpu.VMEM((1, N_HEADS * tq), F32), pltpu.VMEM((KV_LORA, N_HEADS * tq), F32)],
        compiler_params=_cparams("parallel", "arbitrary"),
        name="mla_prompt",
    )(qa, kv, vt, wuv)


def _memkv_kernel(mem_ref, g_ref, w_ref, k_o, v_o):
    kvp = _dot(_rms(mem_ref[...], g_ref[...]).astype(BF16), w_ref[...])
    k_o[...] = kvp[:, 0:MEM_W]
    v_o[...] = kvp[:, MEM_W:2 * MEM_W]


def _mem_kv(mem2, g, w, tm):
    n, d = mem2.shape
    row = lambda wd: pl.BlockSpec((tm, wd), lambda i: (i, 0))
    full = lambda a: pl.BlockSpec(a.shape, lambda i: (0,) * a.ndim)
    return pl.pallas_call(
        _memkv_kernel,
        grid=(n // tm,),
        in_specs=[row(d), full(g), full(w)],
        out_specs=[row(MEM_W), row(MEM_W)],
        out_shape=[jax.ShapeDtypeStruct((n, MEM_W), F32)] * 2,
        compiler_params=_cparams("parallel"),
        name="mem_kv",
    )(mem2, g, w)


def _out_proj(x, cat_scr, wout_ref, gf_ref):
    return _rms(x + _dot(cat_scr[...], wout_ref[...]), gf_ref[...])


def _merge_prompt_kernel(x_ref, ret_ref, mla_ref, mq_ref, sg_ref, mk_ref, mv_ref, wout_ref, gf_ref,
                         y_o, cat_scr):
    cat_scr[:, 0:RET_W] = ret_ref[...] * sg_ref[:, 0:RET_W]
    cat_scr[:, RET_W:2 * RET_W] = mla_ref[...] * sg_ref[:, RET_W:2 * RET_W]
    for h in range(N_HEADS):
        sl = slice(h * HEAD_DIM, (h + 1) * HEAD_DIM)
        s = _dot_nt(mq_ref[:, sl], mk_ref[:, sl].astype(BF16)) * MEM_SCALE
        p = jnp.exp(s - jnp.max(s, axis=-1, keepdims=True))
        p = (p / jnp.sum(p, axis=-1, keepdims=True)).astype(BF16)
        o = _dot(p, mv_ref[:, sl].astype(BF16))
        gsl = slice(2 * RET_W + h * HEAD_DIM, 2 * RET_W + (h + 1) * HEAD_DIM)
        cat_scr[:, gsl] = (o * sg_ref[:, gsl].astype(F32)).astype(BF16)
    y_o[...] = _out_proj(x_ref[...], cat_scr, wout_ref, gf_ref)


def _merge_prompt(x2, ret_n, mla, mq, sg, mk, mv, wout, gf, batch, seq, tm):
    nt = seq // tm
    n_mem = mk.shape[0] // batch
    row = lambda w: pl.BlockSpec((tm, w), lambda b, t: (b * nt + t, 0))
    mem = pl.BlockSpec((n_mem, MEM_W), lambda b, t: (b, 0))
    full = lambda a: pl.BlockSpec(a.shape, lambda b, t: (0,) * a.ndim)
    return pl.pallas_call(
        _merge_prompt_kernel,
        grid=(batch, nt),
        in_specs=[row(x2.shape[1]), row(RET_W), row(RET_W), row(MEM_W), row(D_MIX), mem, mem, full(wout), full(gf)],
        out_specs=row(x2.shape[1]),
        out_shape=jax.ShapeDtypeStruct(x2.shape, F32),
        scratch_shapes=[pltpu.VMEM((tm, D_MIX), BF16)],
        compiler_params=_cparams("parallel", "arbitrary"),
        name="merge_prompt",
    )(x2, ret_n, mla, mq, sg, mk, mv, wout, gf)


def _merge_sample_kernel(x_ref, ret_ref, lat_ref, memo_ref, sg_ref, wuv_ref, wout_ref, gf_ref, y_o, cat_scr):
    cat_scr[:, 0:RET_W] = ret_ref[...] * sg_ref[:, 0:RET_W]
    for h in range(N_HEADS):
        lat = lat_ref[:, h * KV_LORA:(h + 1) * KV_LORA].astype(BF16)
        gsl = slice(RET_W + h * MLA_V, RET_W + (h + 1) * MLA_V)
        cat_scr[:, gsl] = (_dot(lat, wuv_ref[h]) * sg_ref[:, gsl].astype(F32)).astype(BF16)
    cat_scr[:, 2 * RET_W:D_MIX] = (memo_ref[...] * sg_ref[:, 2 * RET_W:D_MIX].astype(F32)).astype(BF16)
    y_o[...] = _out_proj(x_ref[...], cat_scr, wout_ref, gf_ref)


def _merge_sample(x2, ret_n, lat, memo, sg, wuv, wout, gf):
    args = (x2, ret_n, lat, memo, sg, wuv, wout, gf)
    return pl.pallas_call(
        _merge_sample_kernel,
        grid=(1,),
        in_specs=[pl.BlockSpec(a.shape, lambda i, nd=a.ndim: (0,) * nd) for a in args],
        out_specs=pl.BlockSpec(x2.shape, lambda i: (0, 0)),
        out_shape=jax.ShapeDtypeStruct(x2.shape, F32),
        scratch_shapes=[pltpu.VMEM((x2.shape[0], D_MIX), BF16)],
        compiler_params=_cparams("arbitrary"),
        name="merge_sample",
    )(*args)


def _ret_step_kernel(rq_ref, rk_ref, rv_ref, s_ref, gam_ref, gn_ref, ret_o, s_o, *, bt):
    sq = (HEAD_DIM, HEAD_DIM)
    for b in range(bt):
        for h in range(N_HEADS):
            sl = slice(h * HEAD_DIM, (h + 1) * HEAD_DIM)
            q = rq_ref[b:b + 1, sl].astype(F32)
            k = rk_ref[b:b + 1, sl].astype(F32)
            v = rv_ref[b:b + 1, sl].astype(F32)
            gam = gam_ref[h]
            s_old = s_ref[b, h]
            q_col = jnp.transpose(jnp.broadcast_to(q, sq))
            k_col = jnp.transpose(jnp.broadcast_to(k, sq))
            qk = jnp.sum(q * k, axis=-1, keepdims=True)
            o = qk * v + gam * jnp.sum(q_col * s_old, axis=0, keepdims=True)
            s_o[b, h] = s_old * gam + k_col * v
            mu = jnp.mean(o, axis=-1, keepdims=True)
            d = o - mu
            var = jnp.mean(d * d, axis=-1, keepdims=True)
            ret_o[b:b + 1, sl] = (d * lax.rsqrt(var + EPS) * gn_ref[:, sl]).astype(BF16)


def _retention_step(rq, rk, rv, state, gn, bt):
    n = rq.shape[0]
    log_g = jnp.log1p(-jnp.exp2(-5.0 - jnp.arange(N_HEADS, dtype=F32)))
    gam = jnp.broadcast_to(jnp.exp(log_g)[:, None, None], (N_HEADS, 1, HEAD_DIM))
    row = pl.BlockSpec((bt, RET_W), lambda i: (i, 0))
    st = pl.BlockSpec((bt, N_HEADS, HEAD_DIM, HEAD_DIM), lambda i: (i, 0, 0, 0))
    full = lambda a: pl.BlockSpec(a.shape, lambda i: (0,) * a.ndim)
    return pl.pallas_call(
        functools.partial(_ret_step_kernel, bt=bt),
        grid=(n // bt,),
        in_specs=[row, row, row, st, full(gam), full(gn)],
        out_specs=[row, st],
        out_shape=[jax.ShapeDtypeStruct((n, RET_W), BF16), jax.ShapeDtypeStruct(state.shape, F32)],
        compiler_params=_cparams("parallel"),
        name="ret_step",
    )(rq, rk, rv, state, gam, gn)


def _mla_dec_kernel(pt_ref, q_ref, kvn_ref, ckv_hbm, kpet_hbm, o_ref,
                    ckv_buf, kpe_buf, sems, m_scr, l_scr, acc_scr, *, cp, page, n_sub):
    b, j = pl.program_id(0), pl.program_id(1)
    nb, nj = pl.num_programs(0), pl.num_programs(1)
    total = nb * nj
    t = b * nj + j
    slot = t % N_SLOTS
    last = t == total - 1
    ahead = N_SLOTS - 1

    def coords(tt):
        ok = tt < total
        return jnp.where(ok, tt // nj, b), jnp.where(ok, tt % nj, j)

    b_next, j_next = coords(t + ahead)
    slot_next = (t + ahead) % N_SLOTS

    def page_copy(which, bb, jj, sl, r, lookup=True):
        pg = pt_ref[bb, jj * cp + r] if lookup else 0
        if which == 0:
            return pltpu.make_async_copy(ckv_hbm.at[pg], ckv_buf.at[sl, r], sems.at[0, sl])
        return pltpu.make_async_copy(kpet_hbm.at[pg], kpe_buf.at[sl, :, r * page:(r + 1) * page], sems.at[1, sl])

    def page_copies(bb, jj, sl, r, lookup=True):
        return tuple(page_copy(w, bb, jj, sl, r, lookup) for w in (0, 1))

    @pl.when(t == 0)
    def _():
        for tt in range(ahead):
            bb, jj = coords(tt)
            for r in range(cp):
                for cpy in page_copies(bb, jj, tt, r):
                    cpy.start()

    @pl.when(j == 0)
    def _():
        m_scr[...] = jnp.full_like(m_scr, NEG_BIG)
        l_scr[...] = jnp.zeros_like(l_scr)
        acc_scr[...] = jnp.zeros_like(acc_scr)

    q = q_ref[...]
    ql, qp = q[:, 0:KV_LORA], q[:, KV_LORA:KV_LORA + MLA_ROPE]
    c2 = MLA_SCALE * LOG2E

    def update(s_parts, v_loaders, between=None):
        m_old = m_scr[...]
        m_new = m_old
        for s in s_parts:
            m_new = jnp.maximum(m_new, jnp.max(s, axis=-1, keepdims=True))
        alpha = jnp.exp2((m_old - m_new) * c2)
        l = alpha * l_scr[...]
        acc = alpha * acc_scr[...]
        for u, (s, load_v) in enumerate(zip(s_parts, v_loaders)):
            p = jnp.exp2((s - m_new) * c2)
            l = l + jnp.sum(p, axis=-1, keepdims=True)
            acc = acc + _dot(p.astype(BF16), load_v())
            if between is not None:
                between(u)
        m_scr[...] = m_new
        l_scr[...] = l
        acc_scr[...] = acc

    for r in range(cp):
        for cpy in page_copies(b, j, slot, r, lookup=False):
            cpy.wait()
    per = cp // n_sub

    def load_keys(u):
        return ckv_buf[slot, u * per:(u + 1) * per].reshape(per * page, KV_LORA).astype(BF16)

    def start_next(u, which):
        for r in range(u * per, (u + 1) * per):
            page_copy(which, b_next, j_next, slot_next, r).start()

    s_parts = []
    for u in range(n_sub):
        ks = slice(u * per * page, (u + 1) * per * page)
        s_parts.append(_dot_nt(ql, load_keys(u)) + _dot(qp, kpe_buf[slot, :, ks].astype(BF16)))
        start_next(u, 0)
    update(s_parts, [functools.partial(load_keys, u) for u in range(n_sub)],
           between=lambda u: start_next(u, 1))

    @pl.when(j == nj - 1)
    def _():
        kvn = kvn_ref[...].astype(BF16)
        s = _dot_nt(ql, kvn[:, 0:KV_LORA]) + _dot_nt(qp, kvn[:, KV_LORA:KV_LORA + MLA_ROPE])
        s = jnp.where(lax.broadcasted_iota(jnp.int32, s.shape, 1) == 0, s, NEG_BIG)
        update([s], [lambda: kvn[:, 0:KV_LORA]])
        o_ref[...] = acc_scr[...] / l_scr[...]

    @pl.when(last)
    def _():
        for k in range(1, N_SLOTS):
            for r in range(cp):
                for cpy in page_copies(b, j, (t + k) % N_SLOTS, r, lookup=False):
                    cpy.wait()


def _mla_decode(q8, kvn8, pool_ckv, pool_kpet, page_table, cp):
    n, n_pages = page_table.shape
    page = pool_ckv.shape[1]
    qw = q8.shape[-1]
    spec_q = pl.BlockSpec((None, 8, qw), lambda b, j, pt: (b, 0, 0))
    grid_spec = pltpu.PrefetchScalarGridSpec(
        num_scalar_prefetch=1,
        grid=(n, n_pages // cp),
        in_specs=[spec_q, spec_q, pl.BlockSpec(memory_space=pl.ANY), pl.BlockSpec(memory_space=pl.ANY)],
        out_specs=pl.BlockSpec((None, 8, KV_LORA), lambda b, j, pt: (b, 0, 0)),
        scratch_shapes=[pltpu.VMEM((N_SLOTS, cp, page, KV_LORA), F32),
                        pltpu.VMEM((N_SLOTS, MLA_ROPE, cp * page), F32),
                        pltpu.SemaphoreType.DMA((2, N_SLOTS)),
                        pltpu.VMEM((8, 1), F32), pltpu.VMEM((8, 1), F32), pltpu.VMEM((8, KV_LORA), F32)],
    )
    return pl.pallas_call(
        functools.partial(_mla_dec_kernel, cp=cp, page=page, n_sub=1),
        grid_spec=grid_spec,
        out_shape=jax.ShapeDtypeStruct((n, 8, KV_LORA), F32),
        compiler_params=_cparams("arbitrary", "arbitrary"),
        name="mla_decode",
    )(page_table, q8, kvn8, pool_ckv, pool_kpet)


def _mem_dec_kernel(q_ref, mk_ref, mv_ref, o_ref, *, bt):
    n_col = mk_ref.shape[1]
    col_head = jnp.bitwise_and(lax.broadcasted_iota(jnp.int32, (8, n_col), 1), N_HEADS - 1)
    own = col_head == lax.broadcasted_iota(jnp.int32, (8, n_col), 0)
    pad = jnp.zeros((8 - N_HEADS, HEAD_DIM), BF16)
    for b in range(bt):
        q8 = jnp.concatenate([q_ref[b], pad], axis=0)
        s = jnp.where(own, _dot_nt(q8, mk_ref[b].astype(BF16)) * MEM_SCALE, NEG_BIG)
        p = jnp.exp(s - jnp.max(s, axis=-1, keepdims=True))
        p = (p / jnp.sum(p, axis=-1, keepdims=True)).astype(BF16)
        o_ref[b] = _dot(p, mv_ref[b].astype(BF16))[0:N_HEADS, :]


def _mem_decode(mq, mk, mv, bt):
    n, n_col, _ = mk.shape
    row = pl.BlockSpec((bt, N_HEADS, HEAD_DIM), lambda i: (i, 0, 0))
    mem = pl.BlockSpec((bt, n_col, HEAD_DIM), lambda i: (i, 0, 0))
    return pl.pallas_call(
        functools.partial(_mem_dec_kernel, bt=bt),
        grid=(n // bt,),
        in_specs=[row, mem, mem],
        out_specs=row,
        out_shape=jax.ShapeDtypeStruct((n, N_HEADS, HEAD_DIM), F32),
        compiler_params=_cparams("parallel"),
        name="mem_decode",
    )(mq, mk, mv)


def _rope_tables(pos, n_freq, slot):
    inv = ROPE_BASE ** (-jnp.arange(0, 2 * n_freq, 2, dtype=F32) / (2 * n_freq))
    ang = pos.astype(F32)[:, None] * inv[None, :]
    c, s = jnp.cos(ang), jnp.sin(ang)
    z = jnp.zeros((pos.shape[0], slot // 2 - n_freq), F32)
    return jnp.concatenate([c, z, c, z], axis=1), jnp.concatenate([-s, z, s, z], axis=1)


def _spread(w):
    z = jnp.zeros(w.shape[:-1] + (32,), w.dtype)
    return jnp.concatenate([w[..., 0:32], z, w[..., 32:64], z], axis=-1)


def _prep_weights(w_in, w_uq, w_uk, w_uv, w_mem_kv, w_out):
    o = [0, 512, 1024, 1536, 1920, 2176, 2240, 2752, 4288]
    seg = lambda i: w_in[:, o[i]:o[i + 1]]
    win = jnp.concatenate([seg(0), seg(1), seg(2), seg(3), seg(4), seg(6), seg(7), _spread(seg(5))], axis=1)
    wuq = jnp.concatenate([w_uq[:, :, :MLA_NOPE].reshape(Q_LORA, -1),
                           _spread(w_uq[:, :, MLA_NOPE:]).reshape(Q_LORA, -1)], axis=1)
    wuk = jnp.transpose(w_uk, (1, 2, 0))
    wuv = jnp.transpose(w_uv, (1, 0, 2))
    return tuple(a.astype(BF16) for a in (win, wuq, wuk, wuv, w_mem_kv, w_out))


def kernel(x_prompt, x_sample, mem_prompt, cache_ckv, cache_kpe, page_table, state_ret, cache_mem_k, cache_mem_v,
           norm_g, w_in, ret_gn_g, mla_qnorm_g, w_uq, w_uk, mla_kvnorm_g, w_uv, mem_norm_g, w_mem_kv, w_out,
           final_norm_g):
    batch, seq, d_model = x_prompt.shape
    n_dec = x_sample.shape[0]
    n_mem = mem_prompt.shape[1]
    depth = w_in.shape[0]
    assert depth == 1 and x_sample.shape[1] == 1
    l = 0
    win, wuq, wuk, wuv, wmem, wout = _prep_weights(w_in[l], w_uq[l], w_uk[l], w_uv[l], w_mem_kv[l], w_out[l])
    g_in = norm_g[l][None, :]
    g_q = mla_qnorm_g[l][None, :]
    g_kv = mla_kvnorm_g[l][None, :]
    g_gn = ret_gn_g[l][None, :]
    g_mem = mem_norm_g[l][None, :]
    g_fin = final_norm_g[None, :]

    tm = min(512, seq)
    tq = min(256, seq)
    n_pages = page_table.shape[1]
    cp = min(64, n_pages)

    xp = x_prompt.reshape(batch * seq, d_model)
    pos_p = jnp.arange(seq, dtype=jnp.int32)
    tabs_p = _rope_tables(pos_p, HEAD_DIM // 2, LANES) + _rope_tables(pos_p, MLA_ROPE // 2, LANES)
    rq, rk, rv, qa, kvb, ckv, mq, sg, kpet, vt = _project(
        xp, tabs_p, g_in, win, g_q, wuq, wuk, g_kv, batch, seq, tm, tq)
    ret_n, ret_state_p = _retention_prompt(rq, rk, rv, g_gn, batch, seq, tm)
    mla = _mla_prompt(qa, kvb, vt, wuv, batch, seq, tq)
    mk, mv = _mem_kv(mem_prompt.reshape(batch * n_mem, d_model), g_mem, wmem, min(512, batch * n_mem))
    y_p = _merge_prompt(xp, ret_n, mla, mq, sg, mk, mv, wout, g_fin, batch, seq, tm)

    xs = x_sample.reshape(n_dec, d_model)
    pos_s = jnp.full((n_dec,), PAST_LEN, dtype=jnp.int32)
    tabs_s = _rope_tables(pos_s, HEAD_DIM // 2, LANES) + _rope_tables(pos_s, MLA_ROPE // 2, LANES)
    rq_s, rk_s, rv_s, qa_s, _, ckv_s, mq_s, sg_s, kpet_s, _ = _project(
        xs, tabs_s, g_in, win, g_q, wuq, wuk, g_kv, 1, n_dec, n_dec, n_dec)
    kpe_s = jnp.swapaxes(kpet_s[0], 0, 1)
    ret_n_s, ret_state_s = _retention_step(rq_s, rk_s, rv_s, state_ret[l], g_gn, 8)
    qh = qa_s.reshape(n_dec, N_HEADS, QK_W)
    q_std = jnp.concatenate([qh[..., 0:KV_LORA], qh[..., KV_LORA:KV_LORA + 32], qh[..., KV_LORA + 64:KV_LORA + 96]], -1)
    q8 = jnp.pad(q_std, ((0, 0), (0, 8 - N_HEADS), (0, 0)))
    kvn8 = jnp.pad(jnp.concatenate([ckv_s, kpe_s], axis=-1)[:, None, :], ((0, 0), (0, 7), (0, 0)))
    lat_s = _mla_decode(q8, kvn8, cache_ckv[l], jnp.swapaxes(cache_kpe[l], 1, 2), page_table, cp)
    lat_s = lat_s[:, 0:N_HEADS, :].reshape(n_dec, N_HEADS * KV_LORA)
    memo_s = _mem_decode(mq_s.reshape(n_dec, N_HEADS, HEAD_DIM),
                         cache_mem_k[l].reshape(n_dec, n_mem * N_HEADS, HEAD_DIM),
                         cache_mem_v[l].reshape(n_dec, n_mem * N_HEADS, HEAD_DIM), 8).reshape(n_dec, MEM_W)
    y_s = _merge_sample(xs, ret_n_s, lat_s, memo_s, sg_s, wuv, wout, g_fin)

    return (y_p.reshape(batch, seq, d_model), y_s.reshape(n_dec, 1, d_model),
            ckv.reshape(1, batch, seq, KV_LORA), jnp.swapaxes(kpet, 1, 2)[None],
            ret_state_p[None], mk.reshape(1, batch, n_mem, N_HEADS, HEAD_DIM), mv.reshape(1, batch, n_mem, N_HEADS, HEAD_DIM),
            ckv_s.reshape(1, n_dec, 1, KV_LORA), kpe_s.reshape(1, n_dec, 1, MLA_ROPE), ret_state_s[None])
```

```python
import functools

import jax
import jax.numpy as jnp
from jax import lax
from jax.experimental import pallas as pl
from jax.experimental.pallas import tpu as pltpu

F32 = jnp.float32
BF16 = jnp.bfloat16

HEAD_DIM = 128
N_HEADS = 4
RET_W = N_HEADS * HEAD_DIM
MLA_NOPE = 128
MLA_ROPE = 64
MLA_V = 128
Q_LORA = 384
KV_LORA = 256
MEM_W = N_HEADS * HEAD_DIM
D_MIX = 3 * RET_W
RET_CHUNK = 256
PAST_LEN = 16384
ROPE_BASE = 10000.0
EPS = 1e-6
MLA_SCALE = (MLA_NOPE + MLA_ROPE) ** -0.5
MEM_SCALE = HEAD_DIM ** -0.5
RK_SCALE = HEAD_DIM ** -0.5
NEG_BIG = -1e30
LOG2E = 1.4426950408889634
N_SLOTS = 3
N_GROUPS = 2

LANES = 128
QK_W = KV_LORA + LANES
OFF_RQ, OFF_RK, OFF_RV = 0, RET_W, 2 * RET_W
OFF_CQ = 3 * RET_W
OFF_KPE = OFF_CQ + Q_LORA
OFF_CKV = OFF_KPE + LANES
OFF_MQ = OFF_CKV + KV_LORA
OFF_GATE = OFF_MQ + MEM_W
D_IN2 = OFF_GATE + D_MIX

VMEM_LIMIT = 48 * 1024 * 1024


def _cparams(*sem):
    return pltpu.CompilerParams(dimension_semantics=sem, vmem_limit_bytes=VMEM_LIMIT)


def _rms(x, g):
    return x * lax.rsqrt(jnp.mean(x * x, axis=-1, keepdims=True) + EPS) * g


def _dot(a, b):
    return jnp.dot(a, b, preferred_element_type=F32)


def _dot_nt(a, b):
    return lax.dot_general(a, b, (((1,), (1,)), ((), ())), preferred_element_type=F32)


def _rot_half(x, cos, sin):
    return x * cos + pltpu.roll(x, 64, 1) * sin


def _proj_kernel(x_ref, cr_ref, sr_ref, cm_ref, sm_ref, g_ref, win_ref, gq_ref, wuq_ref, wuk_ref, gkv_ref,
                 rq_o, rk_o, rv_o, qa_o, kv_o, ckv_o, mq_o, sg_o, kpet_o, vt_o):
    xn = _rms(x_ref[...], g_ref[...]).astype(BF16)
    cr, sr = cr_ref[...], sr_ref[...]
    cm, sm = cm_ref[...], sm_ref[...]

    seg = lambda off, w: _dot(xn, win_ref[:, off:off + w])
    zck = seg(OFF_CQ, Q_LORA + LANES)
    zcq, zkpe = zck[:, 0:Q_LORA], zck[:, Q_LORA:Q_LORA + LANES]
    gate = seg(OFF_GATE, D_MIX)
    cq = _rms(zcq, gq_ref[...]).astype(BF16)
    zq, zk = seg(OFF_RQ, RET_W), seg(OFF_RK, RET_W)
    q = _dot(cq, wuq_ref[...])
    sg_o[...] = (gate / (1.0 + jnp.exp(-gate))).astype(BF16)
    zckv = seg(OFF_CKV, KV_LORA)
    for h in range(N_HEADS):
        qn = q[:, h * MLA_NOPE:(h + 1) * MLA_NOPE].astype(BF16)
        qa_o[:, h * QK_W:h * QK_W + KV_LORA] = _dot(qn, wuk_ref[h]).astype(BF16)
        qp = q[:, RET_W + h * LANES:RET_W + (h + 1) * LANES]
        qa_o[:, h * QK_W + KV_LORA:(h + 1) * QK_W] = _rot_half(qp, cm, sm).astype(BF16)
    zv, zm = seg(OFF_RV, RET_W), seg(OFF_MQ, MEM_W)
    for h in range(N_HEADS):
        sl = slice(h * HEAD_DIM, (h + 1) * HEAD_DIM)
        rq_o[:, sl] = _rot_half(zq[:, sl], cr, sr).astype(BF16)
        rk_o[:, sl] = (_rot_half(zk[:, sl], cr, sr) * RK_SCALE).astype(BF16)

    ckv = _rms(zckv, gkv_ref[...])
    ckv_o[...] = ckv
    kv_o[:, 0:KV_LORA] = ckv.astype(BF16)
    kp = _rot_half(zkpe, cm, sm)
    kv_o[:, KV_LORA:QK_W] = kp.astype(BF16)
    kpt = jnp.transpose(kp)
    kpet_o[0:32, :] = kpt[0:32, :]
    kpet_o[32:64, :] = kpt[64:96, :]
    vt = jnp.transpose(ckv).astype(BF16)
    tkb = vt_o.shape[-1]
    for u in range(vt_o.shape[0]):
        vt_o[u] = vt[:, u * tkb:(u + 1) * tkb]
    rv_o[...] = zv.astype(BF16)
    mq_o[...] = zm.astype(BF16)


def _project(x2, tabs, g, win, gq, wuq, wuk, gkv, batch, seq, tm, tkb):
    n, d = x2.shape
    nt = seq // tm
    row = lambda w: pl.BlockSpec((tm, w), lambda i: (i, 0))
    tab = pl.BlockSpec((tm, LANES), lambda i: (i % nt, 0))
    full = lambda a: pl.BlockSpec(a.shape, lambda i: (0,) * a.ndim)
    outs = [(RET_W, BF16), (RET_W, BF16), (RET_W, BF16), (N_HEADS * QK_W, BF16), (QK_W, BF16),
            (KV_LORA, F32), (MEM_W, BF16), (D_MIX, BF16)]
    kpet_spec = pl.BlockSpec((None, MLA_ROPE, tm), lambda i: (i // nt, 0, i % nt))
    vt_spec = pl.BlockSpec((None, tm // tkb, KV_LORA, tkb), lambda i: (i // nt, i % nt, 0, 0))
    return pl.pallas_call(
        _proj_kernel,
        grid=(n // tm,),
        in_specs=[row(d), tab, tab, tab, tab, full(g), full(win), full(gq), full(wuq), full(wuk), full(gkv)],
        out_specs=[row(w) for w, _ in outs] + [kpet_spec, vt_spec],
        out_shape=[jax.ShapeDtypeStruct((n, w), dt) for w, dt in outs]
                  + [jax.ShapeDtypeStruct((batch, MLA_ROPE, seq), F32),
                     jax.ShapeDtypeStruct((batch, seq // tkb, KV_LORA, tkb), BF16)],
        compiler_params=_cparams("parallel"),
        name="proj",
    )(x2, *tabs, g, win, gq, wuq, wuk, gkv)


def _ret_kernel(rq_ref, rk_ref, rv_ref, intra_ref, qdec_ref, kdec_ref, sdec_ref, gn_ref,
                ret_o, state_o, s_scr):
    c = pl.program_id(1)

    @pl.when(c == 0)
    def _():
        s_scr[...] = jnp.zeros_like(s_scr)

    chunk = intra_ref.shape[1]
    for ci in range(rq_ref.shape[0] // chunk):
        rs = slice(ci * chunk, (ci + 1) * chunk)
        for h in range(N_HEADS):
            sl = slice(h * HEAD_DIM, (h + 1) * HEAD_DIM)
            q, k, v = rq_ref[rs, sl], rk_ref[rs, sl], rv_ref[rs, sl]
            s_old = s_scr[h]
            sc = _dot_nt(q, k) * intra_ref[h]
            o = _dot(sc.astype(BF16), v) + qdec_ref[h] * _dot(q, s_old.astype(BF16))
            kd_t = jnp.transpose(k.astype(F32) * kdec_ref[h]).astype(BF16)
            s_scr[h] = s_old * sdec_ref[h] + _dot(kd_t, v)
            mu = jnp.mean(o, axis=-1, keepdims=True)
            d = o - mu
            var = jnp.mean(d * d, axis=-1, keepdims=True)
            ret_o[rs, sl] = (d * lax.rsqrt(var + EPS) * gn_ref[:, sl]).astype(BF16)

    @pl.when(c == pl.num_programs(1) - 1)
    def _():
        state_o[...] = s_scr[...]


def _ret_tables(chunk):
    log_g = jnp.log1p(-jnp.exp2(-5.0 - jnp.arange(N_HEADS, dtype=F32)))
    idx = jnp.arange(chunk, dtype=F32)
    diff = idx[:, None] - idx[None, :]
    intra = jnp.where(diff[None] >= 0, jnp.exp(jnp.maximum(diff, 0.0)[None] * log_g[:, None, None]), 0.0)
    q_dec = jnp.exp((idx[None, :] + 1.0) * log_g[:, None])
    k_dec = jnp.exp((chunk - 1.0 - idx)[None, :] * log_g[:, None])
    s_dec = jnp.exp(chunk * log_g)
    bc = lambda a: jnp.broadcast_to(a[:, :, None], (N_HEADS, chunk, HEAD_DIM))
    return intra, bc(q_dec), bc(k_dec), jnp.broadcast_to(s_dec[:, None, None], (N_HEADS, 1, HEAD_DIM))


def _retention_prompt(rq, rk, rv, gn, batch, seq, tr):
    chunk = RET_CHUNK
    assert seq % tr == 0 and tr % chunk == 0
    nc = seq // tr
    intra, qdec, kdec, sdec = _ret_tables(chunk)
    row = pl.BlockSpec((tr, RET_W), lambda b, c: (b * nc + c, 0))
    full = lambda a: pl.BlockSpec(a.shape, lambda b, c: (0,) * a.ndim)
    return pl.pallas_call(
        _ret_kernel,
        grid=(batch, nc),
        in_specs=[row, row, row, full(intra), full(qdec), full(kdec), full(sdec), full(gn)],
        out_specs=[row, pl.BlockSpec((None, N_HEADS, HEAD_DIM, HEAD_DIM), lambda b, c: (b, 0, 0, 0))],
        out_shape=[jax.ShapeDtypeStruct((batch * seq, RET_W), BF16),
                   jax.ShapeDtypeStruct((batch, N_HEADS, HEAD_DIM, HEAD_DIM), F32)],
        scratch_shapes=[pltpu.VMEM((N_HEADS, HEAD_DIM, HEAD_DIM), F32)],
        compiler_params=_cparams("parallel", "arbitrary"),
        name="ret_prompt",
    )(rq, rk, rv, intra, qdec, kdec, sdec, gn)


def _mla_kernel(qa_ref, kv_ref, vt_ref, wuv_ref, mla_o, q_scr, m_scr, l_scr, acc_scr, *, tq):
    i = pl.program_id(1)
    cols = N_HEADS * tq
    c2 = MLA_SCALE * LOG2E
    for h in range(N_HEADS):
        q_scr[h * tq:(h + 1) * tq, :] = qa_ref[:, h * QK_W:(h + 1) * QK_W]
    m_scr[...] = jnp.full_like(m_scr, NEG_BIG)
    l_scr[...] = jnp.zeros_like(l_scr)
    acc_scr[...] = jnp.zeros_like(acc_scr)

    def block(blk, n_sub, masked):
        nk = n_sub * tq
        kj = kv_ref[pl.ds(pl.multiple_of(blk * tq, tq), nk), :]
        gw = cols // N_GROUPS
        if masked:
            kpos = blk * tq + lax.broadcasted_iota(jnp.int32, (nk, gw), 0)
            qpos = i * tq + jnp.bitwise_and(lax.broadcasted_iota(jnp.int32, (nk, gw), 1), tq - 1)
            keep = kpos <= qpos
        scores = [_dot_nt(kj, q_scr[g * gw:(g + 1) * gw, :]) for g in range(N_GROUPS)]
        for g in range(N_GROUPS):
            cs = slice(g * gw, (g + 1) * gw)
            s = scores[g]
            if masked:
                s = jnp.where(keep, s, NEG_BIG)
            m_old = m_scr[:, cs]
            m_new = jnp.maximum(m_old, jnp.max(s, axis=0, keepdims=True))
            alpha = jnp.exp2((m_old - m_new) * c2)
            p = jnp.exp2((s - m_new) * c2)
            l_scr[:, cs] = alpha * l_scr[:, cs] + jnp.sum(p, axis=0, keepdims=True)
            pb = p.astype(BF16)
            pv = _dot(vt_ref[blk], pb[0:tq, :])
            for u in range(1, n_sub):
                pv = pv + _dot(vt_ref[blk + u], pb[u * tq:(u + 1) * tq, :])
            acc_scr[:, cs] = alpha * acc_scr[:, cs] + pv
            m_scr[:, cs] = m_new

    def body(j, carry):
        block(2 * j, 2, False)
        return carry

    lax.fori_loop(0, i // 2, body, 0)

    @pl.when(i % 2 == 1)
    def _():
        block(i - 1, 1, False)

    block(i, 1, True)
    lat_t = acc_scr[...] / l_scr[...]
    for h in range(N_HEADS):
        lat = jnp.transpose(lat_t[:, h * tq:(h + 1) * tq]).astype(BF16)
        mla_o[:, h * MLA_V:(h + 1) * MLA_V] = _dot(lat, wuv_ref[h]).astype(BF16)


def _mla_prompt(qa, kv, vt, wuv, batch, seq, tq):
    nq = seq // tq
    assert vt.shape == (batch, nq, KV_LORA, tq)
    return pl.pallas_call(
        functools.partial(_mla_kernel, tq=tq),
        grid=(batch, nq),
        in_specs=[pl.BlockSpec((tq, N_HEADS * QK_W), lambda b, i: (b * nq + i, 0)),
                  pl.BlockSpec((seq, QK_W), lambda b, i: (b, 0)),
                  pl.BlockSpec((None, nq, KV_LORA, tq), lambda b, i: (b, 0, 0, 0)),
                  pl.BlockSpec(wuv.shape, lambda b, i: (0, 0, 0))],
        out_specs=pl.BlockSpec((tq, N_HEADS * MLA_V), lambda b, i: (b * nq + i, 0)),
        out_shape=jax.ShapeDtypeStruct((batch * seq, N_HEADS * MLA_V), BF16),
        scratch_shapes=[pltpu.VMEM((N_HEADS * tq, QK_W), BF16), pltpu.VMEM((1, N_HEADS * tq), F32),
                        pltpu.VMEM((1, N_HEADS * tq), F32), pltpu.VMEM((KV_LORA, N_HEADS * tq), F32)],
        compiler_params=_cparams("parallel", "arbitrary"),
        name="mla_prompt",
    )(qa, kv, vt, wuv)


def _memkv_kernel(mem_ref, g_ref, w_ref, k_o, v_o):
    kvp = _dot(_rms(mem_ref[...], g_ref[...]).astype(BF16), w_ref[...])
    k_o[...] = kvp[:, 0:MEM_W]
    v_o[...] = kvp[:, MEM_W:2 * MEM_W]


def _mem_kv(mem2, g, w, tm):
    n, d = mem2.shape
    row = lambda wd: pl.BlockSpec((tm, wd), lambda i: (i, 0))
    full = lambda a: pl.BlockSpec(a.shape, lambda i: (0,) * a.ndim)
    return pl.pallas_call(
        _memkv_kernel,
        grid=(n // tm,),
        in_specs=[row(d), full(g), full(w)],
        out_specs=[row(MEM_W), row(MEM_W)],
        out_shape=[jax.ShapeDtypeStruct((n, MEM_W), F32)] * 2,
        compiler_params=_cparams("parallel"),
        name="mem_kv",
    )(mem2, g, w)


def _out_proj(x, cat_scr, wout_ref, gf_ref):
    return _rms(x + _dot(cat_scr[...], wout_ref[...]), gf_ref[...])


def _merge_prompt_kernel(x_ref, ret_ref, mla_ref, mq_ref, sg_ref, mk_ref, mv_ref, wout_ref, gf_ref,
                         y_o, cat_scr):
    cat_scr[:, 0:RET_W] = ret_ref[...] * sg_ref[:, 0:RET_W]
    cat_scr[:, RET_W:2 * RET_W] = mla_ref[...] * sg_ref[:, RET_W:2 * RET_W]
    heads = [slice(h * HEAD_DIM, (h + 1) * HEAD_DIM) for h in range(N_HEADS)]
    scores = [_dot_nt(mq_ref[:, sl], mk_ref[:, sl].astype(BF16)) for sl in heads]
    part = x_ref[...] + _dot(cat_scr[:, 0:2 * RET_W], wout_ref[0:2 * RET_W, :])
    for h, sl in enumerate(heads):
        s = scores[h] * MEM_SCALE
        p = jnp.exp(s - jnp.max(s, axis=-1, keepdims=True))
        p = (p / jnp.sum(p, axis=-1, keepdims=True)).astype(BF16)
        o = _dot(p, mv_ref[:, sl].astype(BF16))
        gsl = slice(2 * RET_W + h * HEAD_DIM, 2 * RET_W + (h + 1) * HEAD_DIM)
        cat_scr[:, gsl] = (o * sg_ref[:, gsl].astype(F32)).astype(BF16)
    y_o[...] = _rms(part + _dot(cat_scr[:, 2 * RET_W:D_MIX], wout_ref[2 * RET_W:D_MIX, :]), gf_ref[...])


def _merge_prompt(x2, ret_n, mla, mq, sg, mk, mv, wout, gf, batch, seq, tm):
    nt = seq // tm
    n_mem = mk.shape[0] // batch
    row = lambda w: pl.BlockSpec((tm, w), lambda b, t: (b * nt + t, 0))
    mem = pl.BlockSpec((n_mem, MEM_W), lambda b, t: (b, 0))
    full = lambda a: pl.BlockSpec(a.shape, lambda b, t: (0,) * a.ndim)
    return pl.pallas_call(
        _merge_prompt_kernel,
        grid=(batch, nt),
        in_specs=[row(x2.shape[1]), row(RET_W), row(RET_W), row(MEM_W), row(D_MIX), mem, mem, full(wout), full(gf)],
        out_specs=row(x2.shape[1]),
        out_shape=jax.ShapeDtypeStruct(x2.shape, F32),
        scratch_shapes=[pltpu.VMEM((tm, D_MIX), BF16)],
        compiler_params=_cparams("parallel", "arbitrary"),
        name="merge_prompt",
    )(x2, ret_n, mla, mq, sg, mk, mv, wout, gf)


def _merge_sample_kernel(x_ref, ret_ref, lat_ref, memo_ref, sg_ref, wuv_ref, wout_ref, gf_ref, y_o, cat_scr):
    cat_scr[:, 0:RET_W] = ret_ref[...] * sg_ref[:, 0:RET_W]
    for h in range(N_HEADS):
        lat = lat_ref[:, h * KV_LORA:(h + 1) * KV_LORA].astype(BF16)
        gsl = slice(RET_W + h * MLA_V, RET_W + (h + 1) * MLA_V)
        cat_scr[:, gsl] = (_dot(lat, wuv_ref[h]) * sg_ref[:, gsl].astype(F32)).astype(BF16)
    cat_scr[:, 2 * RET_W:D_MIX] = (memo_ref[...] * sg_ref[:, 2 * RET_W:D_MIX].astype(F32)).astype(BF16)
    y_o[...] = _out_proj(x_ref[...], cat_scr, wout_ref, gf_ref)


def _merge_sample(x2, ret_n, lat, memo, sg, wuv, wout, gf):
    args = (x2, ret_n, lat, memo, sg, wuv, wout, gf)
    return pl.pallas_call(
        _merge_sample_kernel,
        grid=(1,),
        in_specs=[pl.BlockSpec(a.shape, lambda i, nd=a.ndim: (0,) * nd) for a in args],
        out_specs=pl.BlockSpec(x2.shape, lambda i: (0, 0)),
        out_shape=jax.ShapeDtypeStruct(x2.shape, F32),
        scratch_shapes=[pltpu.VMEM((x2.shape[0], D_MIX), BF16)],
        compiler_params=_cparams("arbitrary"),
        name="merge_sample",
    )(*args)


def _ret_step_kernel(rq_ref, rk_ref, rv_ref, s_ref, gam_ref, gn_ref, ret_o, s_o, *, bt):
    sq = (HEAD_DIM, HEAD_DIM)
    for b in range(bt):
        for h in range(N_HEADS):
            sl = slice(h * HEAD_DIM, (h + 1) * HEAD_DIM)
            q = rq_ref[b:b + 1, sl].astype(F32)
            k = rk_ref[b:b + 1, sl].astype(F32)
            v = rv_ref[b:b + 1, sl].astype(F32)
            gam = gam_ref[h]
            s_old = s_ref[b, h]
            q_col = jnp.transpose(jnp.broadcast_to(q, sq))
            k_col = jnp.transpose(jnp.broadcast_to(k, sq))
            qk = jnp.sum(q * k, axis=-1, keepdims=True)
            o = qk * v + gam * jnp.sum(q_col * s_old, axis=0, keepdims=True)
            s_o[b, h] = s_old * gam + k_col * v
            mu = jnp.mean(o, axis=-1, keepdims=True)
            d = o - mu
            var = jnp.mean(d * d, axis=-1, keepdims=True)
            ret_o[b:b + 1, sl] = (d * lax.rsqrt(var + EPS) * gn_ref[:, sl]).astype(BF16)


def _retention_step(rq, rk, rv, state, gn, bt):
    n = rq.shape[0]
    log_g = jnp.log1p(-jnp.exp2(-5.0 - jnp.arange(N_HEADS, dtype=F32)))
    gam = jnp.broadcast_to(jnp.exp(log_g)[:, None, None], (N_HEADS, 1, HEAD_DIM))
    row = pl.BlockSpec((bt, RET_W), lambda i: (i, 0))
    st = pl.BlockSpec((bt, N_HEADS, HEAD_DIM, HEAD_DIM), lambda i: (i, 0, 0, 0))
    full = lambda a: pl.BlockSpec(a.shape, lambda i: (0,) * a.ndim)
    return pl.pallas_call(
        functools.partial(_ret_step_kernel, bt=bt),
        grid=(n // bt,),
        in_specs=[row, row, row, st, full(gam), full(gn)],
        out_specs=[row, st],
        out_shape=[jax.ShapeDtypeStruct((n, RET_W), BF16), jax.ShapeDtypeStruct(state.shape, F32)],
        compiler_params=_cparams("parallel"),
        name="ret_step",
    )(rq, rk, rv, state, gam, gn)


def _mla_dec_kernel(pt_ref, q_ref, kvn_ref, ckv_hbm, kpet_hbm, o_ref,
                    ckv_buf, kpe_buf, sems, m_scr, l_scr, acc_scr, *, cp, page, n_sub):
    b, j = pl.program_id(0), pl.program_id(1)
    nb, nj = pl.num_programs(0), pl.num_programs(1)
    total = nb * nj
    t = b * nj + j
    slot = t % N_SLOTS
    last = t == total - 1
    ahead = N_SLOTS - 1

    def coords(tt):
        ok = tt < total
        return jnp.where(ok, tt // nj, b), jnp.where(ok, tt % nj, j)

    b_next, j_next = coords(t + ahead)
    slot_next = (t + ahead) % N_SLOTS

    def page_copy(which, bb, jj, sl, r, lookup=True):
        pg = pt_ref[bb, jj * cp + r] if lookup else 0
        if which == 0:
            return pltpu.make_async_copy(ckv_hbm.at[pg], ckv_buf.at[sl, r], sems.at[0, sl])
        return pltpu.make_async_copy(kpet_hbm.at[pg], kpe_buf.at[sl, :, r * page:(r + 1) * page], sems.at[1, sl])

    def page_copies(bb, jj, sl, r, lookup=True):
        return tuple(page_copy(w, bb, jj, sl, r, lookup) for w in (0, 1))

    @pl.when(t == 0)
    def _():
        for tt in range(ahead):
            bb, jj = coords(tt)
            for r in range(cp):
                for cpy in page_copies(bb, jj, tt, r):
                    cpy.start()

    @pl.when(j == 0)
    def _():
        m_scr[...] = jnp.full_like(m_scr, NEG_BIG)
        l_scr[...] = jnp.zeros_like(l_scr)
        acc_scr[...] = jnp.zeros_like(acc_scr)

    q = q_ref[...]
    ql, qp = q[:, 0:KV_LORA], q[:, KV_LORA:KV_LORA + MLA_ROPE]
    c2 = MLA_SCALE * LOG2E

    def update(s_parts, v_loaders, between=None):
        m_old = m_scr[...]
        m_new = m_old
        for s in s_parts:
            m_new = jnp.maximum(m_new, jnp.max(s, axis=-1, keepdims=True))
        alpha = jnp.exp2((m_old - m_new) * c2)
        l = alpha * l_scr[...]
        acc = alpha * acc_scr[...]
        for u, (s, load_v) in enumerate(zip(s_parts, v_loaders)):
            p = jnp.exp2((s - m_new) * c2)
            l = l + jnp.sum(p, axis=-1, keepdims=True)
            acc = acc + _dot(p.astype(BF16), load_v())
            if between is not None:
                between(u)
        m_scr[...] = m_new
        l_scr[...] = l
        acc_scr[...] = acc

    for r in range(cp):
        for cpy in page_copies(b, j, slot, r, lookup=False):
            cpy.wait()
    per = cp // n_sub

    def load_keys(u):
        return ckv_buf[slot, u * per:(u + 1) * per].reshape(per * page, KV_LORA).astype(BF16)

    def start_next(u, which):
        for r in range(u * per, (u + 1) * per):
            page_copy(which, b_next, j_next, slot_next, r).start()

    s_parts = []
    for u in range(n_sub):
        ks = slice(u * per * page, (u + 1) * per * page)
        s_parts.append(_dot_nt(ql, load_keys(u)) + _dot(qp, kpe_buf[slot, :, ks].astype(BF16)))
        start_next(u, 0)
    update(s_parts, [functools.partial(load_keys, u) for u in range(n_sub)],
           between=lambda u: start_next(u, 1))

    @pl.when(j == nj - 1)
    def _():
        kvn = kvn_ref[...].astype(BF16)
        s = _dot_nt(ql, kvn[:, 0:KV_LORA]) + _dot_nt(qp, kvn[:, KV_LORA:KV_LORA + MLA_ROPE])
        s = jnp.where(lax.broadcasted_iota(jnp.int32, s.shape, 1) == 0, s, NEG_BIG)
        update([s], [lambda: kvn[:, 0:KV_LORA]])
        o_ref[...] = acc_scr[...] / l_scr[...]

    @pl.when(last)
    def _():
        for k in range(1, N_SLOTS):
            for r in range(cp):
                for cpy in page_copies(b, j, (t + k) % N_SLOTS, r, lookup=False):
                    cpy.wait()


def _mla_decode(q8, kvn8, pool_ckv, pool_kpet, page_table, cp):
    n, n_pages = page_table.shape
    page = pool_ckv.shape[1]
    qw = q8.shape[-1]
    spec_q = pl.BlockSpec((None, 8, qw), lambda b, j, pt: (b, 0, 0))
    grid_spec = pltpu.PrefetchScalarGridSpec(
        num_scalar_prefetch=1,
        grid=(n, n_pages // cp),
        in_specs=[spec_q, spec_q, pl.BlockSpec(memory_space=pl.ANY), pl.BlockSpec(memory_space=pl.ANY)],
        out_specs=pl.BlockSpec((None, 8, KV_LORA), lambda b, j, pt: (b, 0, 0)),
        scratch_shapes=[pltpu.VMEM((N_SLOTS, cp, page, KV_LORA), F32),
                        pltpu.VMEM((N_SLOTS, MLA_ROPE, cp * page), F32),
                        pltpu.SemaphoreType.DMA((2, N_SLOTS)),
                        pltpu.VMEM((8, 1), F32), pltpu.VMEM((8, 1), F32), pltpu.VMEM((8, KV_LORA), F32)],
    )
    return pl.pallas_call(
        functools.partial(_mla_dec_kernel, cp=cp, page=page, n_sub=1),
        grid_spec=grid_spec,
        out_shape=jax.ShapeDtypeStruct((n, 8, KV_LORA), F32),
        compiler_params=_cparams("arbitrary", "arbitrary"),
        name="mla_decode",
    )(page_table, q8, kvn8, pool_ckv, pool_kpet)


def _mem_dec_kernel(q_ref, mk_ref, mv_ref, o_ref, *, bt):
    n_col = mk_ref.shape[1]
    col_head = jnp.bitwise_and(lax.broadcasted_iota(jnp.int32, (8, n_col), 1), N_HEADS - 1)
    own = col_head == lax.broadcasted_iota(jnp.int32, (8, n_col), 0)
    pad = jnp.zeros((8 - N_HEADS, HEAD_DIM), BF16)
    for b in range(bt):
        q8 = jnp.concatenate([q_ref[b], pad], axis=0)
        s = jnp.where(own, _dot_nt(q8, mk_ref[b].astype(BF16)) * MEM_SCALE, NEG_BIG)
        p = jnp.exp(s - jnp.max(s, axis=-1, keepdims=True))
        p = (p / jnp.sum(p, axis=-1, keepdims=True)).astype(BF16)
        o_ref[b] = _dot(p, mv_ref[b].astype(BF16))[0:N_HEADS, :]


def _mem_decode(mq, mk, mv, bt):
    n, n_col, _ = mk.shape
    row = pl.BlockSpec((bt, N_HEADS, HEAD_DIM), lambda i: (i, 0, 0))
    mem = pl.BlockSpec((bt, n_col, HEAD_DIM), lambda i: (i, 0, 0))
    return pl.pallas_call(
        functools.partial(_mem_dec_kernel, bt=bt),
        grid=(n // bt,),
        in_specs=[row, mem, mem],
        out_specs=row,
        out_shape=jax.ShapeDtypeStruct((n, N_HEADS, HEAD_DIM), F32),
        compiler_params=_cparams("parallel"),
        name="mem_decode",
    )(mq, mk, mv)


def _rope_tables(pos, n_freq, slot):
    inv = ROPE_BASE ** (-jnp.arange(0, 2 * n_freq, 2, dtype=F32) / (2 * n_freq))
    ang = pos.astype(F32)[:, None] * inv[None, :]
    c, s = jnp.cos(ang), jnp.sin(ang)
    z = jnp.zeros((pos.shape[0], slot // 2 - n_freq), F32)
    return jnp.concatenate([c, z, c, z], axis=1), jnp.concatenate([-s, z, s, z], axis=1)


def _spread(w):
    z = jnp.zeros(w.shape[:-1] + (32,), w.dtype)
    return jnp.concatenate([w[..., 0:32], z, w[..., 32:64], z], axis=-1)


def _prep_weights(w_in, w_uq, w_uk, w_uv, w_mem_kv, w_out):
    o = [0, 512, 1024, 1536, 1920, 2176, 2240, 2752, 4288]
    seg = lambda i: w_in[:, o[i]:o[i + 1]]
    win = jnp.concatenate([seg(0), seg(1), seg(2), seg(3), _spread(seg(5)), seg(4), seg(6), seg(7)], axis=1)
    wuq = jnp.concatenate([w_uq[:, :, :MLA_NOPE].reshape(Q_LORA, -1),
                           _spread(w_uq[:, :, MLA_NOPE:]).reshape(Q_LORA, -1)], axis=1)
    wuk = jnp.transpose(w_uk, (1, 2, 0))
    wuv = jnp.transpose(w_uv, (1, 0, 2))
    return tuple(a.astype(BF16) for a in (win, wuq, wuk, wuv, w_mem_kv, w_out))


def kernel(x_prompt, x_sample, mem_prompt, cache_ckv, cache_kpe, page_table, state_ret, cache_mem_k, cache_mem_v,
           norm_g, w_in, ret_gn_g, mla_qnorm_g, w_uq, w_uk, mla_kvnorm_g, w_uv, mem_norm_g, w_mem_kv, w_out,
           final_norm_g):
    batch, seq, d_model = x_prompt.shape
    n_dec = x_sample.shape[0]
    n_mem = mem_prompt.shape[1]
    depth = w_in.shape[0]
    assert depth == 1 and x_sample.shape[1] == 1
    l = 0
    win, wuq, wuk, wuv, wmem, wout = _prep_weights(w_in[l], w_uq[l], w_uk[l], w_uv[l], w_mem_kv[l], w_out[l])
    g_in = norm_g[l][None, :]
    g_q = mla_qnorm_g[l][None, :]
    g_kv = mla_kvnorm_g[l][None, :]
    g_gn = ret_gn_g[l][None, :]
    g_mem = mem_norm_g[l][None, :]
    g_fin = final_norm_g[None, :]

    tm = min(512, seq)
    tq = min(256, seq)
    n_pages = page_table.shape[1]
    cp = min(64, n_pages)

    xp = x_prompt.reshape(batch * seq, d_model)
    pos_p = jnp.arange(seq, dtype=jnp.int32)
    tabs_p = _rope_tables(pos_p, HEAD_DIM // 2, LANES) + _rope_tables(pos_p, MLA_ROPE // 2, LANES)
    rq, rk, rv, qa, kvb, ckv, mq, sg, kpet, vt = _project(
        xp, tabs_p, g_in, win, g_q, wuq, wuk, g_kv, batch, seq, tm, tq)
    ret_n, ret_state_p = _retention_prompt(rq, rk, rv, g_gn, batch, seq, tm)
    mla = _mla_prompt(qa, kvb, vt, wuv, batch, seq, tq)
    mk, mv = _mem_kv(mem_prompt.reshape(batch * n_mem, d_model), g_mem, wmem, min(512, batch * n_mem))
    y_p = _merge_prompt(xp, ret_n, mla, mq, sg, mk, mv, wout, g_fin, batch, seq, tm)

    xs = x_sample.reshape(n_dec, d_model)
    pos_s = jnp.full((n_dec,), PAST_LEN, dtype=jnp.int32)
    tabs_s = _rope_tables(pos_s, HEAD_DIM // 2, LANES) + _rope_tables(pos_s, MLA_ROPE // 2, LANES)
    rq_s, rk_s, rv_s, qa_s, _, ckv_s, mq_s, sg_s, kpet_s, _ = _project(
        xs, tabs_s, g_in, win, g_q, wuq, wuk, g_kv, 1, n_dec, n_dec, n_dec)
    kpe_s = jnp.swapaxes(kpet_s[0], 0, 1)
    ret_n_s, ret_state_s = _retention_step(rq_s, rk_s, rv_s, state_ret[l], g_gn, 8)
    qh = qa_s.reshape(n_dec, N_HEADS, QK_W)
    q_std = jnp.concatenate([qh[..., 0:KV_LORA], qh[..., KV_LORA:KV_LORA + 32], qh[..., KV_LORA + 64:KV_LORA + 96]], -1)
    q8 = jnp.pad(q_std, ((0, 0), (0, 8 - N_HEADS), (0, 0)))
    kvn8 = jnp.pad(jnp.concatenate([ckv_s, kpe_s], axis=-1)[:, None, :], ((0, 0), (0, 7), (0, 0)))
    lat_s = _mla_decode(q8, kvn8, cache_ckv[l], jnp.swapaxes(cache_kpe[l], 1, 2), page_table, cp)
    lat_s = lat_s[:, 0:N_HEADS, :].reshape(n_dec, N_HEADS * KV_LORA)
    memo_s = _mem_decode(mq_s.reshape(n_dec, N_HEADS, HEAD_DIM),
                         cache_mem_k[l].reshape(n_dec, n_mem * N_HEADS, HEAD_DIM),
                         cache_mem_v[l].reshape(n_dec, n_mem * N_HEADS, HEAD_DIM), 8).reshape(n_dec, MEM_W)
    y_s = _merge_sample(xs, ret_n_s, lat_s, memo_s, sg_s, wuv, wout, g_fin)

    return (y_p.reshape(batch, seq, d_model), y_s.reshape(n_dec, 1, d_model),
            ckv.reshape(1, batch, seq, KV_LORA), jnp.swapaxes(kpet, 1, 2)[None],
            ret_state_p[None], mk.reshape(1, batch, n_mem, N_HEADS, HEAD_DIM), mv.reshape(1, batch, n_mem, N_HEADS, HEAD_DIM),
            ckv_s.reshape(1, n_dec, 1, KV_LORA), kpe_s.reshape(1, n_dec, 1, MLA_ROPE), ret_state_s[None])
```

```python
import functools

import jax
import jax.numpy as jnp
from jax import lax
from jax.experimental import pallas as pl
from jax.experimental.pallas import tpu as pltpu

F32 = jnp.float32
BF16 = jnp.bfloat16

HEAD_DIM = 128
N_HEADS = 4
RET_W = N_HEADS * HEAD_DIM
MLA_NOPE = 128
MLA_ROPE = 64
MLA_V = 128
Q_LORA = 384
KV_LORA = 256
MEM_W = N_HEADS * HEAD_DIM
D_MIX = 3 * RET_W
RET_CHUNK = 256
PAST_LEN = 16384
ROPE_BASE = 10000.0
EPS = 1e-6
MLA_SCALE = (MLA_NOPE + MLA_ROPE) ** -0.5
MEM_SCALE = HEAD_DIM ** -0.5
RK_SCALE = HEAD_DIM ** -0.5
NEG_BIG = -1e30
LOG2E = 1.4426950408889634
N_SLOTS = 3
MLA_SUB = 256

LANES = 128
QK_W = KV_LORA + LANES
OFF_RQ, OFF_RK, OFF_RV = 0, RET_W, 2 * RET_W
OFF_CQ = 3 * RET_W
OFF_KPE = OFF_CQ + Q_LORA
OFF_CKV = OFF_KPE + LANES
OFF_MQ = OFF_CKV + KV_LORA
OFF_GATE = OFF_MQ + MEM_W
D_IN2 = OFF_GATE + D_MIX

VMEM_LIMIT = 48 * 1024 * 1024


def _cparams(*sem):
    return pltpu.CompilerParams(dimension_semantics=sem, vmem_limit_bytes=VMEM_LIMIT)


def _rms(x, g):
    return x * lax.rsqrt(jnp.mean(x * x, axis=-1, keepdims=True) + EPS) * g


def _dot(a, b):
    return jnp.dot(a, b, preferred_element_type=F32)


def _dot_nt(a, b):
    return lax.dot_general(a, b, (((1,), (1,)), ((), ())), preferred_element_type=F32)


def _rot_half(x, cos, sin):
    return x * cos + pltpu.roll(x, 64, 1) * sin


def _proj_kernel(x_ref, cr_ref, sr_ref, cm_ref, sm_ref, g_ref, win_ref, gq_ref, wuq_ref, wuk_ref, gkv_ref,
                 rq_o, rk_o, rv_o, qa_o, kv_o, ckv_o, mq_o, sg_o, kpet_o, vt_o):
    xn = _rms(x_ref[...], g_ref[...]).astype(BF16)
    cr, sr = cr_ref[...], sr_ref[...]
    cm, sm = cm_ref[...], sm_ref[...]

    seg = lambda off, w: _dot(xn, win_ref[:, off:off + w])
    zck = seg(OFF_CQ, Q_LORA + LANES)
    zcq, zkpe = zck[:, 0:Q_LORA], zck[:, Q_LORA:Q_LORA + LANES]
    gate = seg(OFF_GATE, D_MIX)
    cq = _rms(zcq, gq_ref[...]).astype(BF16)
    zq, zk = seg(OFF_RQ, RET_W), seg(OFF_RK, RET_W)
    q = _dot(cq, wuq_ref[...])
    sg_o[...] = (gate / (1.0 + jnp.exp(-gate))).astype(BF16)
    zckv = seg(OFF_CKV, KV_LORA)
    for h in range(N_HEADS):
        qn = q[:, h * MLA_NOPE:(h + 1) * MLA_NOPE].astype(BF16)
        qa_o[:, h * QK_W:h * QK_W + KV_LORA] = _dot(qn, wuk_ref[h]).astype(BF16)
        qp = q[:, RET_W + h * LANES:RET_W + (h + 1) * LANES]
        qa_o[:, h * QK_W + KV_LORA:(h + 1) * QK_W] = _rot_half(qp, cm, sm).astype(BF16)
    zv, zm = seg(OFF_RV, RET_W), seg(OFF_MQ, MEM_W)
    for h in range(N_HEADS):
        sl = slice(h * HEAD_DIM, (h + 1) * HEAD_DIM)
        rq_o[:, sl] = _rot_half(zq[:, sl], cr, sr).astype(BF16)
        rk_o[:, sl] = (_rot_half(zk[:, sl], cr, sr) * RK_SCALE).astype(BF16)

    ckv = _rms(zckv, gkv_ref[...])
    ckv_o[...] = ckv
    kv_o[:, 0:KV_LORA] = ckv.astype(BF16)
    kp = _rot_half(zkpe, cm, sm)
    kv_o[:, KV_LORA:QK_W] = kp.astype(BF16)
    kpt = jnp.transpose(kp)
    kpet_o[0:32, :] = kpt[0:32, :]
    kpet_o[32:64, :] = kpt[64:96, :]
    vt = jnp.transpose(ckv).astype(BF16)
    tkb = vt_o.shape[-1]
    for u in range(vt_o.shape[0]):
        vt_o[u] = vt[:, u * tkb:(u + 1) * tkb]
    rv_o[...] = zv.astype(BF16)
    mq_o[...] = zm.astype(BF16)


def _project(x2, tabs, g, win, gq, wuq, wuk, gkv, batch, seq, tm, tkb):
    n, d = x2.shape
    nt = seq // tm
    row = lambda w: pl.BlockSpec((tm, w), lambda i: (i, 0))
    tab = pl.BlockSpec((tm, LANES), lambda i: (i % nt, 0))
    full = lambda a: pl.BlockSpec(a.shape, lambda i: (0,) * a.ndim)
    outs = [(RET_W, BF16), (RET_W, BF16), (RET_W, BF16), (N_HEADS * QK_W, BF16), (QK_W, BF16),
            (KV_LORA, F32), (MEM_W, BF16), (D_MIX, BF16)]
    kpet_spec = pl.BlockSpec((None, MLA_ROPE, tm), lambda i: (i // nt, 0, i % nt))
    vt_spec = pl.BlockSpec((None, tm // tkb, KV_LORA, tkb), lambda i: (i // nt, i % nt, 0, 0))
    return pl.pallas_call(
        _proj_kernel,
        grid=(n // tm,),
        in_specs=[row(d), tab, tab, tab, tab, full(g), full(win), full(gq), full(wuq), full(wuk), full(gkv)],
        out_specs=[row(w) for w, _ in outs] + [kpet_spec, vt_spec],
        out_shape=[jax.ShapeDtypeStruct((n, w), dt) for w, dt in outs]
                  + [jax.ShapeDtypeStruct((batch, MLA_ROPE, seq), F32),
                     jax.ShapeDtypeStruct((batch, seq // tkb, KV_LORA, tkb), BF16)],
        compiler_params=_cparams("parallel"),
        name="proj",
    )(x2, *tabs, g, win, gq, wuq, wuk, gkv)


def _ret_kernel(rq_ref, rk_ref, rv_ref, intra_ref, qdec_ref, kdec_ref, sdec_ref, gn_ref,
                ret_o, state_o, s_scr):
    c = pl.program_id(1)

    @pl.when(c == 0)
    def _():
        s_scr[...] = jnp.zeros_like(s_scr)

    chunk = intra_ref.shape[1]
    for ci in range(rq_ref.shape[0] // chunk):
        rs = slice(ci * chunk, (ci + 1) * chunk)
        heads = [slice(h * HEAD_DIM, (h + 1) * HEAD_DIM) for h in range(N_HEADS)]
        qk = [_dot_nt(rq_ref[rs, sl], rk_ref[rs, sl]) for sl in heads]
        qs = [_dot(rq_ref[rs, sl], s_scr[h].astype(BF16)) for h, sl in enumerate(heads)]
        for h, sl in enumerate(heads):
            k, v = rk_ref[rs, sl], rv_ref[rs, sl]
            o = _dot((qk[h] * intra_ref[h]).astype(BF16), v) + qdec_ref[h] * qs[h]
            kd_t = jnp.transpose(k.astype(F32) * kdec_ref[h]).astype(BF16)
            s_scr[h] = s_scr[h] * sdec_ref[h] + _dot(kd_t, v)
            mu = jnp.mean(o, axis=-1, keepdims=True)
            d = o - mu
            var = jnp.mean(d * d, axis=-1, keepdims=True)
            ret_o[rs, sl] = (d * lax.rsqrt(var + EPS) * gn_ref[:, sl]).astype(BF16)

    @pl.when(c == pl.num_programs(1) - 1)
    def _():
        state_o[...] = s_scr[...]


def _ret_tables(chunk):
    log_g = jnp.log1p(-jnp.exp2(-5.0 - jnp.arange(N_HEADS, dtype=F32)))
    idx = jnp.arange(chunk, dtype=F32)
    diff = idx[:, None] - idx[None, :]
    intra = jnp.where(diff[None] >= 0, jnp.exp(jnp.maximum(diff, 0.0)[None] * log_g[:, None, None]), 0.0)
    q_dec = jnp.exp((idx[None, :] + 1.0) * log_g[:, None])
    k_dec = jnp.exp((chunk - 1.0 - idx)[None, :] * log_g[:, None])
    s_dec = jnp.exp(chunk * log_g)
    bc = lambda a: jnp.broadcast_to(a[:, :, None], (N_HEADS, chunk, HEAD_DIM))
    return intra, bc(q_dec), bc(k_dec), jnp.broadcast_to(s_dec[:, None, None], (N_HEADS, 1, HEAD_DIM))


def _retention_prompt(rq, rk, rv, gn, batch, seq, tr):
    chunk = RET_CHUNK
    assert seq % tr == 0 and tr % chunk == 0
    nc = seq // tr
    intra, qdec, kdec, sdec = _ret_tables(chunk)
    row = pl.BlockSpec((tr, RET_W), lambda b, c: (b * nc + c, 0))
    full = lambda a: pl.BlockSpec(a.shape, lambda b, c: (0,) * a.ndim)
    return pl.pallas_call(
        _ret_kernel,
        grid=(batch, nc),
        in_specs=[row, row, row, full(intra), full(qdec), full(kdec), full(sdec), full(gn)],
        out_specs=[row, pl.BlockSpec((None, N_HEADS, HEAD_DIM, HEAD_DIM), lambda b, c: (b, 0, 0, 0))],
        out_shape=[jax.ShapeDtypeStruct((batch * seq, RET_W), BF16),
                   jax.ShapeDtypeStruct((batch, N_HEADS, HEAD_DIM, HEAD_DIM), F32)],
        scratch_shapes=[pltpu.VMEM((N_HEADS, HEAD_DIM, HEAD_DIM), F32)],
        compiler_params=_cparams("parallel", "arbitrary"),
        name="ret_prompt",
    )(rq, rk, rv, intra, qdec, kdec, sdec, gn)


def _mla_kernel(qa_ref, kv_ref, vt_ref, wuv_ref, mla_o, q_scr, m_scr, l_scr, acc_scr, *, tq):
    i = pl.program_id(1)
    cols = N_HEADS * tq
    c2 = MLA_SCALE * LOG2E
    for h in range(N_HEADS):
        q_scr[h * tq:(h + 1) * tq, :] = qa_ref[:, h * QK_W:(h + 1) * QK_W]
    m_scr[...] = jnp.full_like(m_scr, NEG_BIG)
    l_scr[...] = jnp.zeros_like(l_scr)
    acc_scr[...] = jnp.zeros_like(acc_scr)

    def block(blk, off, nk, masked, q_lo=0):
        kj = kv_ref[pl.ds(pl.multiple_of(blk * tq + off, MLA_SUB), nk), :]
        nq = tq - q_lo
        if masked:
            kpos = blk * tq + off + lax.broadcasted_iota(jnp.int32, (nk, nq), 0)
            qpos = i * tq + q_lo + lax.broadcasted_iota(jnp.int32, (nk, nq), 1)
            keep = kpos <= qpos
        scores = [_dot_nt(kj, q_scr[h * tq + q_lo:(h + 1) * tq, :]) for h in range(N_HEADS)]
        for h in range(N_HEADS):
            cs = slice(h * tq + q_lo, (h + 1) * tq)
            s = scores[h]
            if masked:
                s = jnp.where(keep, s, NEG_BIG)
            m_old = m_scr[:, cs]
            m_new = jnp.maximum(m_old, jnp.max(s, axis=0, keepdims=True))
            alpha = jnp.exp2((m_old - m_new) * c2)
            p = jnp.exp2((s - m_new) * c2)
            l_scr[:, cs] = alpha * l_scr[:, cs] + jnp.sum(p, axis=0, keepdims=True)
            pb = p.astype(BF16)
            pv = None
            for u in range(0, nk, tq):
                w = min(tq, nk - u)
                part = _dot(vt_ref[blk + u // tq, :, off:off + w], pb[u:u + w, :])
                pv = part if pv is None else pv + part
            acc_scr[:, cs] = alpha * acc_scr[:, cs] + pv
            m_scr[:, cs] = m_new

    def body(j, carry):
        block(2 * j, 0, 2 * tq, False)
        return carry

    lax.fori_loop(0, i // 2, body, 0)

    @pl.when(i % 2 == 1)
    def _():
        block(i - 1, 0, tq, False)

    for u in range(tq // MLA_SUB):
        block(i, u * MLA_SUB, MLA_SUB, True, q_lo=u * MLA_SUB)
    lat_t = acc_scr[...] / l_scr[...]
    for h in range(N_HEADS):
        lat = jnp.transpose(lat_t[:, h * tq:(h + 1) * tq]).astype(BF16)
        mla_o[:, h * MLA_V:(h + 1) * MLA_V] = _dot(lat, wuv_ref[h]).astype(BF16)


def _mla_prompt(qa, kv, vt, wuv, batch, seq, tq):
    nq = seq // tq
    assert vt.shape == (batch, nq, KV_LORA, tq)
    return pl.pallas_call(
        functools.partial(_mla_kernel, tq=tq),
        grid=(batch, nq),
        in_specs=[pl.BlockSpec((tq, N_HEADS * QK_W), lambda b, i: (b * nq + i, 0)),
                  pl.BlockSpec((seq, QK_W), lambda b, i: (b, 0)),
                  pl.BlockSpec((None, nq, KV_LORA, tq), lambda b, i: (b, 0, 0, 0)),
                  pl.BlockSpec(wuv.shape, lambda b, i: (0, 0, 0))],
        out_specs=pl.BlockSpec((tq, N_HEADS * MLA_V), lambda b, i: (b * nq + i, 0)),
        out_shape=jax.ShapeDtypeStruct((batch * seq, N_HEADS * MLA_V), BF16),
        scratch_shapes=[pltpu.VMEM((N_HEADS * tq, QK_W), BF16), pltpu.VMEM((1, N_HEADS * tq), F32),
                        pltpu.VMEM((1, N_HEADS * tq), F32), pltpu.VMEM((KV_LORA, N_HEADS * tq), F32)],
        compiler_params=_cparams("parallel", "arbitrary"),
        name="mla_prompt",
    )(qa, kv, vt, wuv)


def _memkv_kernel(mem_ref, g_ref, w_ref, k_o, v_o, k4_o, v4_o):
    kvp = _dot(_rms(mem_ref[...], g_ref[...]).astype(BF16), w_ref[...])
    k_o[...] = kvp[:, 0:MEM_W]
    v_o[...] = kvp[:, MEM_W:2 * MEM_W]
    tm = mem_ref.shape[0]
    for h in range(N_HEADS):
        k4_o[pl.ds(h, tm, stride=N_HEADS), :] = kvp[:, h * HEAD_DIM:(h + 1) * HEAD_DIM]
        v4_o[pl.ds(h, tm, stride=N_HEADS), :] = kvp[:, MEM_W + h * HEAD_DIM:MEM_W + (h + 1) * HEAD_DIM]


def _mem_kv(mem2, g, w, tm):
    n, d = mem2.shape
    row = lambda wd: pl.BlockSpec((tm, wd), lambda i: (i, 0))
    row4 = pl.BlockSpec((tm * N_HEADS, HEAD_DIM), lambda i: (i, 0))
    full = lambda a: pl.BlockSpec(a.shape, lambda i: (0,) * a.ndim)
    return pl.pallas_call(
        _memkv_kernel,
        grid=(n // tm,),
        in_specs=[row(d), full(g), full(w)],
        out_specs=[row(MEM_W), row(MEM_W), row4, row4],
        out_shape=[jax.ShapeDtypeStruct((n, MEM_W), F32)] * 2
                  + [jax.ShapeDtypeStruct((n * N_HEADS, HEAD_DIM), F32)] * 2,
        compiler_params=_cparams("parallel"),
        name="mem_kv",
    )(mem2, g, w)


def _out_proj(x, cat_scr, wout_ref, gf_ref):
    return _rms(x + _dot(cat_scr[...], wout_ref[...]), gf_ref[...])


def _merge_prompt_kernel(x_ref, ret_ref, mla_ref, mq_ref, sg_ref, mk_ref, mv_ref, wout_ref, gf_ref,
                         y_o, cat_scr):
    cat_scr[:, 0:RET_W] = ret_ref[...] * sg_ref[:, 0:RET_W]
    cat_scr[:, RET_W:2 * RET_W] = mla_ref[...] * sg_ref[:, RET_W:2 * RET_W]
    heads = [slice(h * HEAD_DIM, (h + 1) * HEAD_DIM) for h in range(N_HEADS)]
    scores = [_dot_nt(mq_ref[:, sl], mk_ref[:, sl].astype(BF16)) for sl in heads]
    part = x_ref[...] + _dot(cat_scr[:, 0:2 * RET_W], wout_ref[0:2 * RET_W, :])
    for h, sl in enumerate(heads):
        s = scores[h] * MEM_SCALE
        p = jnp.exp(s - jnp.max(s, axis=-1, keepdims=True))
        p = (p / jnp.sum(p, axis=-1, keepdims=True)).astype(BF16)
        o = _dot(p, mv_ref[:, sl].astype(BF16))
        gsl = slice(2 * RET_W + h * HEAD_DIM, 2 * RET_W + (h + 1) * HEAD_DIM)
        cat_scr[:, gsl] = (o * sg_ref[:, gsl].astype(F32)).astype(BF16)
    y_o[...] = _rms(part + _dot(cat_scr[:, 2 * RET_W:D_MIX], wout_ref[2 * RET_W:D_MIX, :]), gf_ref[...])


def _merge_prompt(x2, ret_n, mla, mq, sg, mk, mv, wout, gf, batch, seq, tm):
    nt = seq // tm
    n_mem = mk.shape[0] // batch
    row = lambda w: pl.BlockSpec((tm, w), lambda b, t: (b * nt + t, 0))
    mem = pl.BlockSpec((n_mem, MEM_W), lambda b, t: (b, 0))
    full = lambda a: pl.BlockSpec(a.shape, lambda b, t: (0,) * a.ndim)
    return pl.pallas_call(
        _merge_prompt_kernel,
        grid=(batch, nt),
        in_specs=[row(x2.shape[1]), row(RET_W), row(RET_W), row(MEM_W), row(D_MIX), mem, mem, full(wout), full(gf)],
        out_specs=row(x2.shape[1]),
        out_shape=jax.ShapeDtypeStruct(x2.shape, F32),
        scratch_shapes=[pltpu.VMEM((tm, D_MIX), BF16)],
        compiler_params=_cparams("parallel", "arbitrary"),
        name="merge_prompt",
    )(x2, ret_n, mla, mq, sg, mk, mv, wout, gf)


def _merge_sample_kernel(x_ref, ret_ref, lat_ref, memo_ref, sg_ref, wuv_ref, wout_ref, gf_ref, y_o, cat_scr):
    cat_scr[:, 0:RET_W] = ret_ref[...] * sg_ref[:, 0:RET_W]
    for h in range(N_HEADS):
        lat = lat_ref[:, h * KV_LORA:(h + 1) * KV_LORA].astype(BF16)
        gsl = slice(RET_W + h * MLA_V, RET_W + (h + 1) * MLA_V)
        cat_scr[:, gsl] = (_dot(lat, wuv_ref[h]) * sg_ref[:, gsl].astype(F32)).astype(BF16)
    cat_scr[:, 2 * RET_W:D_MIX] = (memo_ref[...] * sg_ref[:, 2 * RET_W:D_MIX].astype(F32)).astype(BF16)
    y_o[...] = _out_proj(x_ref[...], cat_scr, wout_ref, gf_ref)


def _merge_sample(x2, ret_n, lat, memo, sg, wuv, wout, gf):
    args = (x2, ret_n, lat, memo, sg, wuv, wout, gf)
    return pl.pallas_call(
        _merge_sample_kernel,
        grid=(1,),
        in_specs=[pl.BlockSpec(a.shape, lambda i, nd=a.ndim: (0,) * nd) for a in args],
        out_specs=pl.BlockSpec(x2.shape, lambda i: (0, 0)),
        out_shape=jax.ShapeDtypeStruct(x2.shape, F32),
        scratch_shapes=[pltpu.VMEM((x2.shape[0], D_MIX), BF16)],
        compiler_params=_cparams("arbitrary"),
        name="merge_sample",
    )(*args)


def _ret_step_kernel(rq_ref, rk_ref, rv_ref, s_ref, gam_ref, gn_ref, ret_o, s_o, *, bt):
    sq = (HEAD_DIM, HEAD_DIM)
    for b in range(bt):
        for h in range(N_HEADS):
            sl = slice(h * HEAD_DIM, (h + 1) * HEAD_DIM)
            q = rq_ref[b:b + 1, sl].astype(F32)
            k = rk_ref[b:b + 1, sl].astype(F32)
            v = rv_ref[b:b + 1, sl].astype(F32)
            gam = gam_ref[h]
            s_old = s_ref[b, h]
            q_col = jnp.transpose(jnp.broadcast_to(q, sq))
            k_col = jnp.transpose(jnp.broadcast_to(k, sq))
            qk = jnp.sum(q * k, axis=-1, keepdims=True)
            o = qk * v + gam * jnp.sum(q_col * s_old, axis=0, keepdims=True)
            s_o[b, h] = s_old * gam + k_col * v
            mu = jnp.mean(o, axis=-1, keepdims=True)
            d = o - mu
            var = jnp.mean(d * d, axis=-1, keepdims=True)
            ret_o[b:b + 1, sl] = (d * lax.rsqrt(var + EPS) * gn_ref[:, sl]).astype(BF16)


def _retention_step(rq, rk, rv, state, gn, bt):
    n = rq.shape[0]
    log_g = jnp.log1p(-jnp.exp2(-5.0 - jnp.arange(N_HEADS, dtype=F32)))
    gam = jnp.broadcast_to(jnp.exp(log_g)[:, None, None], (N_HEADS, 1, HEAD_DIM))
    row = pl.BlockSpec((bt, RET_W), lambda i: (i, 0))
    st = pl.BlockSpec((bt, N_HEADS, HEAD_DIM, HEAD_DIM), lambda i: (i, 0, 0, 0))
    full = lambda a: pl.BlockSpec(a.shape, lambda i: (0,) * a.ndim)
    return pl.pallas_call(
        functools.partial(_ret_step_kernel, bt=bt),
        grid=(n // bt,),
        in_specs=[row, row, row, st, full(gam), full(gn)],
        out_specs=[row, st],
        out_shape=[jax.ShapeDtypeStruct((n, RET_W), BF16), jax.ShapeDtypeStruct(state.shape, F32)],
        compiler_params=_cparams("parallel"),
        name="ret_step",
    )(rq, rk, rv, state, gam, gn)


def _mla_dec_kernel(pt_ref, q_ref, kvn_ref, ckv_hbm, kpet_hbm, o_ref,
                    ckv_buf, kpe_buf, sems, m_scr, l_scr, acc_scr, *, cp, page, n_sub):
    b, j = pl.program_id(0), pl.program_id(1)
    nb, nj = pl.num_programs(0), pl.num_programs(1)
    total = nb * nj
    t = b * nj + j
    slot = t % N_SLOTS
    last = t == total - 1
    ahead = N_SLOTS - 1

    def coords(tt):
        ok = tt < total
        return jnp.where(ok, tt // nj, b), jnp.where(ok, tt % nj, j)

    b_next, j_next = coords(t + ahead)
    slot_next = (t + ahead) % N_SLOTS

    def page_copy(which, bb, jj, sl, r, lookup=True):
        pg = pt_ref[bb, jj * cp + r] if lookup else 0
        if which == 0:
            return pltpu.make_async_copy(ckv_hbm.at[pg], ckv_buf.at[sl, r], sems.at[0, sl])
        return pltpu.make_async_copy(kpet_hbm.at[pg], kpe_buf.at[sl, :, r * page:(r + 1) * page], sems.at[1, sl])

    def page_copies(bb, jj, sl, r, lookup=True):
        return tuple(page_copy(w, bb, jj, sl, r, lookup) for w in (0, 1))

    @pl.when(t == 0)
    def _():
        for tt in range(ahead):
            bb, jj = coords(tt)
            for r in range(cp):
                for cpy in page_copies(bb, jj, tt, r):
                    cpy.start()

    @pl.when(j == 0)
    def _():
        m_scr[...] = jnp.full_like(m_scr, NEG_BIG)
        l_scr[...] = jnp.zeros_like(l_scr)
        acc_scr[...] = jnp.zeros_like(acc_scr)

    q = q_ref[...]
    ql, qp = q[:, 0:KV_LORA], q[:, KV_LORA:KV_LORA + MLA_ROPE]
    c2 = MLA_SCALE * LOG2E

    def update(s_parts, v_loaders, between=None):
        m_old = m_scr[...]
        m_new = m_old
        for s in s_parts:
            m_new = jnp.maximum(m_new, jnp.max(s, axis=-1, keepdims=True))
        alpha = jnp.exp2((m_old - m_new) * c2)
        l = alpha * l_scr[...]
        acc = alpha * acc_scr[...]
        for u, (s, load_v) in enumerate(zip(s_parts, v_loaders)):
            p = jnp.exp2((s - m_new) * c2)
            l = l + jnp.sum(p, axis=-1, keepdims=True)
            acc = acc + _dot(p.astype(BF16), load_v())
            if between is not None:
                between(u)
        m_scr[...] = m_new
        l_scr[...] = l
        acc_scr[...] = acc

    for r in range(cp):
        for cpy in page_copies(b, j, slot, r, lookup=False):
            cpy.wait()
    per = cp // n_sub

    def load_keys(u):
        return ckv_buf[slot, u * per:(u + 1) * per].reshape(per * page, KV_LORA).astype(BF16)

    def start_next(u, which):
        for r in range(u * per, (u + 1) * per):
            page_copy(which, b_next, j_next, slot_next, r).start()

    s_parts = []
    for u in range(n_sub):
        ks = slice(u * per * page, (u + 1) * per * page)
        s_parts.append(_dot_nt(ql, load_keys(u)) + _dot(qp, kpe_buf[slot, :, ks].astype(BF16)))
        start_next(u, 0)
    update(s_parts, [functools.partial(load_keys, u) for u in range(n_sub)],
           between=lambda u: start_next(u, 1))

    @pl.when(j == nj - 1)
    def _():
        kvn = kvn_ref[...].astype(BF16)
        s = _dot_nt(ql, kvn[:, 0:KV_LORA]) + _dot_nt(qp, kvn[:, KV_LORA:KV_LORA + MLA_ROPE])
        s = jnp.where(lax.broadcasted_iota(jnp.int32, s.shape, 1) == 0, s, NEG_BIG)
        update([s], [lambda: kvn[:, 0:KV_LORA]])
        o_ref[...] = acc_scr[...] / l_scr[...]

    @pl.when(last)
    def _():
        for k in range(1, N_SLOTS):
            for r in range(cp):
                for cpy in page_copies(b, j, (t + k) % N_SLOTS, r, lookup=False):
                    cpy.wait()


def _mla_decode(q8, kvn8, pool_ckv, pool_kpet, page_table, cp):
    n, n_pages = page_table.shape
    page = pool_ckv.shape[1]
    qw = q8.shape[-1]
    spec_q = pl.BlockSpec((None, 8, qw), lambda b, j, pt: (b, 0, 0))
    grid_spec = pltpu.PrefetchScalarGridSpec(
        num_scalar_prefetch=1,
        grid=(n, n_pages // cp),
        in_specs=[spec_q, spec_q, pl.BlockSpec(memory_space=pl.ANY), pl.BlockSpec(memory_space=pl.ANY)],
        out_specs=pl.BlockSpec((None, 8, KV_LORA), lambda b, j, pt: (b, 0, 0)),
        scratch_shapes=[pltpu.VMEM((N_SLOTS, cp, page, KV_LORA), F32),
                        pltpu.VMEM((N_SLOTS, MLA_ROPE, cp * page), F32),
                        pltpu.SemaphoreType.DMA((2, N_SLOTS)),
                        pltpu.VMEM((8, 1), F32), pltpu.VMEM((8, 1), F32), pltpu.VMEM((8, KV_LORA), F32)],
    )
    return pl.pallas_call(
        functools.partial(_mla_dec_kernel, cp=cp, page=page, n_sub=1),
        grid_spec=grid_spec,
        out_shape=jax.ShapeDtypeStruct((n, 8, KV_LORA), F32),
        compiler_params=_cparams("arbitrary", "arbitrary"),
        name="mla_decode",
    )(page_table, q8, kvn8, pool_ckv, pool_kpet)


def _mem_dec_kernel(q_ref, mk_ref, mv_ref, o_ref, *, bt):
    n_col = mk_ref.shape[1]
    col_head = jnp.bitwise_and(lax.broadcasted_iota(jnp.int32, (8, n_col), 1), N_HEADS - 1)
    own = col_head == lax.broadcasted_iota(jnp.int32, (8, n_col), 0)
    pad = jnp.zeros((8 - N_HEADS, HEAD_DIM), BF16)
    for b in range(bt):
        q8 = jnp.concatenate([q_ref[b], pad], axis=0)
        s = jnp.where(own, _dot_nt(q8, mk_ref[b].astype(BF16)) * MEM_SCALE, NEG_BIG)
        p = jnp.exp(s - jnp.max(s, axis=-1, keepdims=True))
        p = (p / jnp.sum(p, axis=-1, keepdims=True)).astype(BF16)
        o_ref[b] = _dot(p, mv_ref[b].astype(BF16))[0:N_HEADS, :]


def _mem_decode(mq, mk, mv, bt):
    n, n_col, _ = mk.shape
    row = pl.BlockSpec((bt, N_HEADS, HEAD_DIM), lambda i: (i, 0, 0))
    mem = pl.BlockSpec((bt, n_col, HEAD_DIM), lambda i: (i, 0, 0))
    return pl.pallas_call(
        functools.partial(_mem_dec_kernel, bt=bt),
        grid=(n // bt,),
        in_specs=[row, mem, mem],
        out_specs=row,
        out_shape=jax.ShapeDtypeStruct((n, N_HEADS, HEAD_DIM), F32),
        compiler_params=_cparams("parallel"),
        name="mem_decode",
    )(mq, mk, mv)


def _rope_tables(pos, n_freq, slot):
    inv = ROPE_BASE ** (-jnp.arange(0, 2 * n_freq, 2, dtype=F32) / (2 * n_freq))
    ang = pos.astype(F32)[:, None] * inv[None, :]
    c, s = jnp.cos(ang), jnp.sin(ang)
    z = jnp.zeros((pos.shape[0], slot // 2 - n_freq), F32)
    return jnp.concatenate([c, z, c, z], axis=1), jnp.concatenate([-s, z, s, z], axis=1)


def _spread(w):
    z = jnp.zeros(w.shape[:-1] + (32,), w.dtype)
    return jnp.concatenate([w[..., 0:32], z, w[..., 32:64], z], axis=-1)


def _prep_weights(w_in, w_uq, w_uk, w_uv, w_mem_kv, w_out):
    o = [0, 512, 1024, 1536, 1920, 2176, 2240, 2752, 4288]
    seg = lambda i: w_in[:, o[i]:o[i + 1]]
    win = jnp.concatenate([seg(0), seg(1), seg(2), seg(3), _spread(seg(5)), seg(4), seg(6), seg(7)], axis=1)
    wuq = jnp.concatenate([w_uq[:, :, :MLA_NOPE].reshape(Q_LORA, -1),
                           _spread(w_uq[:, :, MLA_NOPE:]).reshape(Q_LORA, -1)], axis=1)
    wuk = jnp.transpose(w_uk, (1, 2, 0))
    wuv = jnp.transpose(w_uv, (1, 0, 2))
    return tuple(a.astype(BF16) for a in (win, wuq, wuk, wuv, w_mem_kv, w_out))


def kernel(x_prompt, x_sample, mem_prompt, cache_ckv, cache_kpe, page_table, state_ret, cache_mem_k, cache_mem_v,
           norm_g, w_in, ret_gn_g, mla_qnorm_g, w_uq, w_uk, mla_kvnorm_g, w_uv, mem_norm_g, w_mem_kv, w_out,
           final_norm_g):
    batch, seq, d_model = x_prompt.shape
    n_dec = x_sample.shape[0]
    n_mem = mem_prompt.shape[1]
    depth = w_in.shape[0]
    assert depth == 1 and x_sample.shape[1] == 1
    l = 0
    win, wuq, wuk, wuv, wmem, wout = _prep_weights(w_in[l], w_uq[l], w_uk[l], w_uv[l], w_mem_kv[l], w_out[l])
    g_in = norm_g[l][None, :]
    g_q = mla_qnorm_g[l][None, :]
    g_kv = mla_kvnorm_g[l][None, :]
    g_gn = ret_gn_g[l][None, :]
    g_mem = mem_norm_g[l][None, :]
    g_fin = final_norm_g[None, :]

    tm = min(512, seq)
    tq = min(512, seq)
    n_pages = page_table.shape[1]
    cp = min(64, n_pages)

    xp = x_prompt.reshape(batch * seq, d_model)
    pos_p = jnp.arange(seq, dtype=jnp.int32)
    tabs_p = _rope_tables(pos_p, HEAD_DIM // 2, LANES) + _rope_tables(pos_p, MLA_ROPE // 2, LANES)
    rq, rk, rv, qa, kvb, ckv, mq, sg, kpet, vt = _project(
        xp, tabs_p, g_in, win, g_q, wuq, wuk, g_kv, batch, seq, tm, tq)
    ret_n, ret_state_p = _retention_prompt(rq, rk, rv, g_gn, batch, seq, tm)
    mla = _mla_prompt(qa, kvb, vt, wuv, batch, seq, tq)
    mk, mv, mk4, mv4 = _mem_kv(mem_prompt.reshape(batch * n_mem, d_model), g_mem, wmem, min(512, batch * n_mem))
    y_p = _merge_prompt(xp, ret_n, mla, mq, sg, mk, mv, wout, g_fin, batch, seq, tm)

    xs = x_sample.reshape(n_dec, d_model)
    pos_s = jnp.full((n_dec,), PAST_LEN, dtype=jnp.int32)
    tabs_s = _rope_tables(pos_s, HEAD_DIM // 2, LANES) + _rope_tables(pos_s, MLA_ROPE // 2, LANES)
    rq_s, rk_s, rv_s, qa_s, _, ckv_s, mq_s, sg_s, kpet_s, _ = _project(
        xs, tabs_s, g_in, win, g_q, wuq, wuk, g_kv, 1, n_dec, n_dec, n_dec)
    kpe_s = jnp.swapaxes(kpet_s[0], 0, 1)
    ret_n_s, ret_state_s = _retention_step(rq_s, rk_s, rv_s, state_ret[l], g_gn, 8)
    qh = qa_s.reshape(n_dec, N_HEADS, QK_W)
    q_std = jnp.concatenate([qh[..., 0:KV_LORA], qh[..., KV_LORA:KV_LORA + 32], qh[..., KV_LORA + 64:KV_LORA + 96]], -1)
    q8 = jnp.pad(q_std, ((0, 0), (0, 8 - N_HEADS), (0, 0)))
    kvn8 = jnp.pad(jnp.concatenate([ckv_s, kpe_s], axis=-1)[:, None, :], ((0, 0), (0, 7), (0, 0)))
    lat_s = _mla_decode(q8, kvn8, cache_ckv[l], jnp.swapaxes(cache_kpe[l], 1, 2), page_table, cp)
    lat_s = lat_s[:, 0:N_HEADS, :].reshape(n_dec, N_HEADS * KV_LORA)
    memo_s = _mem_decode(mq_s.reshape(n_dec, N_HEADS, HEAD_DIM),
                         cache_mem_k[l].reshape(n_dec, n_mem * N_HEADS, HEAD_DIM),
                         cache_mem_v[l].reshape(n_dec, n_mem * N_HEADS, HEAD_DIM), 8).reshape(n_dec, MEM_W)
    y_s = _merge_sample(xs, ret_n_s, lat_s, memo_s, sg_s, wuv, wout, g_fin)

    return (y_p.reshape(batch, seq, d_model), y_s.reshape(n_dec, 1, d_model),
            ckv.reshape(1, batch, seq, KV_LORA), jnp.swapaxes(kpet, 1, 2)[None],
            ret_state_p[None], mk4.reshape(1, batch, n_mem, N_HEADS, HEAD_DIM), mv4.reshape(1, batch, n_mem, N_HEADS, HEAD_DIM),
            ckv_s.reshape(1, n_dec, 1, KV_LORA), kpe_s.reshape(1, n_dec, 1, MLA_ROPE), ret_state_s[None])
```

```python
import functools

import jax
import jax.numpy as jnp
from jax import lax
from jax.experimental import pallas as pl
from jax.experimental.pallas import tpu as pltpu

F32 = jnp.float32
BF16 = jnp.bfloat16

HEAD_DIM = 128
N_HEADS = 4
RET_W = N_HEADS * HEAD_DIM
MLA_NOPE = 128
MLA_ROPE = 64
MLA_V = 128
Q_LORA = 384
KV_LORA = 256
MEM_W = N_HEADS * HEAD_DIM
D_MIX = 3 * RET_W
RET_CHUNK = 256
PAST_LEN = 16384
ROPE_BASE = 10000.0
EPS = 1e-6
MLA_SCALE = (MLA_NOPE + MLA_ROPE) ** -0.5
MEM_SCALE = HEAD_DIM ** -0.5
RK_SCALE = HEAD_DIM ** -0.5
NEG_BIG = -1e30
LOG2E = 1.4426950408889634
N_SLOTS = 3
MLA_SUB = 256

LANES = 128
QK_W = KV_LORA + LANES
OFF_RQ, OFF_RK, OFF_RV = 0, RET_W, 2 * RET_W
OFF_CQ = 3 * RET_W
OFF_KPE = OFF_CQ + Q_LORA
OFF_CKV = OFF_KPE + LANES
OFF_MQ = OFF_CKV + KV_LORA
OFF_GATE = OFF_MQ + MEM_W
D_IN2 = OFF_GATE + D_MIX

VMEM_LIMIT = 48 * 1024 * 1024


def _cparams(*sem):
    return pltpu.CompilerParams(dimension_semantics=sem, vmem_limit_bytes=VMEM_LIMIT)


def _rms(x, g):
    return x * lax.rsqrt(jnp.mean(x * x, axis=-1, keepdims=True) + EPS) * g


def _dot(a, b):
    return jnp.dot(a, b, preferred_element_type=F32)


def _dot_nt(a, b):
    return lax.dot_general(a, b, (((1,), (1,)), ((), ())), preferred_element_type=F32)


def _rot_half(x, cos, sin):
    return x * cos + pltpu.roll(x, 64, 1) * sin


def _proj_kernel(x_ref, cr_ref, sr_ref, cm_ref, sm_ref, g_ref, win_ref, gq_ref, wuq_ref, wuk_ref, gkv_ref,
                 rq_o, rk_o, rv_o, qa_o, kv_o, ckv_o, mq_o, sg_o, kpet_o, vt_o):
    xn = _rms(x_ref[...], g_ref[...]).astype(BF16)
    cr, sr = cr_ref[...], sr_ref[...]
    cm, sm = cm_ref[...], sm_ref[...]

    seg = lambda off, w: _dot(xn, win_ref[:, off:off + w])
    zck = seg(OFF_CQ, Q_LORA + LANES)
    zcq, zkpe = zck[:, 0:Q_LORA], zck[:, Q_LORA:Q_LORA + LANES]
    gate = seg(OFF_GATE, D_MIX)
    cq = _rms(zcq, gq_ref[...]).astype(BF16)
    zq, zk = seg(OFF_RQ, RET_W), seg(OFF_RK, RET_W)
    q = _dot(cq, wuq_ref[...])
    sg_o[...] = (gate / (1.0 + jnp.exp(-gate))).astype(BF16)
    zckv = seg(OFF_CKV, KV_LORA)
    for h in range(N_HEADS):
        qn = q[:, h * MLA_NOPE:(h + 1) * MLA_NOPE].astype(BF16)
        qa_o[:, h * QK_W:h * QK_W + KV_LORA] = _dot(qn, wuk_ref[h]).astype(BF16)
        qp = q[:, RET_W + h * LANES:RET_W + (h + 1) * LANES]
        qa_o[:, h * QK_W + KV_LORA:(h + 1) * QK_W] = _rot_half(qp, cm, sm).astype(BF16)
    zv, zm = seg(OFF_RV, RET_W), seg(OFF_MQ, MEM_W)
    for h in range(N_HEADS):
        sl = slice(h * HEAD_DIM, (h + 1) * HEAD_DIM)
        rq_o[:, sl] = _rot_half(zq[:, sl], cr, sr).astype(BF16)
        rk_o[:, sl] = (_rot_half(zk[:, sl], cr, sr) * RK_SCALE).astype(BF16)

    ckv = _rms(zckv, gkv_ref[...])
    ckv_o[...] = ckv
    kv_o[:, 0:KV_LORA] = ckv.astype(BF16)
    kp = _rot_half(zkpe, cm, sm)
    kv_o[:, KV_LORA:QK_W] = kp.astype(BF16)
    kpt = jnp.transpose(kp)
    kpet_o[0:32, :] = kpt[0:32, :]
    kpet_o[32:64, :] = kpt[64:96, :]
    vt = jnp.transpose(ckv).astype(BF16)
    tkb = vt_o.shape[-1]
    for u in range(vt_o.shape[0]):
        vt_o[u] = vt[:, u * tkb:(u + 1) * tkb]
    rv_o[...] = zv.astype(BF16)
    mq_o[...] = zm.astype(BF16)


def _project(x2, tabs, g, win, gq, wuq, wuk, gkv, batch, seq, tm, tkb):
    n, d = x2.shape
    nt = seq // tm
    row = lambda w: pl.BlockSpec((tm, w), lambda i: (i, 0))
    tab = pl.BlockSpec((tm, LANES), lambda i: (i % nt, 0))
    full = lambda a: pl.BlockSpec(a.shape, lambda i: (0,) * a.ndim)
    outs = [(RET_W, BF16), (RET_W, BF16), (RET_W, BF16), (N_HEADS * QK_W, BF16), (QK_W, BF16),
            (KV_LORA, F32), (MEM_W, BF16), (D_MIX, BF16)]
    kpet_spec = pl.BlockSpec((None, MLA_ROPE, tm), lambda i: (i // nt, 0, i % nt))
    vt_spec = pl.BlockSpec((None, tm // tkb, KV_LORA, tkb), lambda i: (i // nt, i % nt, 0, 0))
    return pl.pallas_call(
        _proj_kernel,
        grid=(n // tm,),
        in_specs=[row(d), tab, tab, tab, tab, full(g), full(win), full(gq), full(wuq), full(wuk), full(gkv)],
        out_specs=[row(w) for w, _ in outs] + [kpet_spec, vt_spec],
        out_shape=[jax.ShapeDtypeStruct((n, w), dt) for w, dt in outs]
                  + [jax.ShapeDtypeStruct((batch, MLA_ROPE, seq), F32),
                     jax.ShapeDtypeStruct((batch, seq // tkb, KV_LORA, tkb), BF16)],
        compiler_params=_cparams("parallel"),
        name="proj",
    )(x2, *tabs, g, win, gq, wuq, wuk, gkv)


def _ret_kernel(rq_ref, rk_ref, rv_ref, intra_ref, qdec_ref, kdec_ref, sdec_ref, gn_ref,
                ret_o, state_o, s_scr):
    c = pl.program_id(1)

    @pl.when(c == 0)
    def _():
        s_scr[...] = jnp.zeros_like(s_scr)

    chunk = intra_ref.shape[1]
    for ci in range(rq_ref.shape[0] // chunk):
        rs = slice(ci * chunk, (ci + 1) * chunk)
        heads = [slice(h * HEAD_DIM, (h + 1) * HEAD_DIM) for h in range(N_HEADS)]
        qk = [_dot_nt(rq_ref[rs, sl], rk_ref[rs, sl]) for sl in heads]
        qs = [_dot(rq_ref[rs, sl], s_scr[h].astype(BF16)) for h, sl in enumerate(heads)]
        for h, sl in enumerate(heads):
            k, v = rk_ref[rs, sl], rv_ref[rs, sl]
            o = _dot((qk[h] * intra_ref[h]).astype(BF16), v) + qdec_ref[h] * qs[h]
            kd_t = jnp.transpose(k.astype(F32) * kdec_ref[h]).astype(BF16)
            s_scr[h] = s_scr[h] * sdec_ref[h] + _dot(kd_t, v)
            mu = jnp.mean(o, axis=-1, keepdims=True)
            d = o - mu
            var = jnp.mean(d * d, axis=-1, keepdims=True)
            ret_o[rs, sl] = (d * lax.rsqrt(var + EPS) * gn_ref[:, sl]).astype(BF16)

    @pl.when(c == pl.num_programs(1) - 1)
    def _():
        state_o[...] = s_scr[...]


def _ret_tables(chunk):
    log_g = jnp.log1p(-jnp.exp2(-5.0 - jnp.arange(N_HEADS, dtype=F32)))
    idx = jnp.arange(chunk, dtype=F32)
    diff = idx[:, None] - idx[None, :]
    intra = jnp.where(diff[None] >= 0, jnp.exp(jnp.maximum(diff, 0.0)[None] * log_g[:, None, None]), 0.0)
    q_dec = jnp.exp((idx[None, :] + 1.0) * log_g[:, None])
    k_dec = jnp.exp((chunk - 1.0 - idx)[None, :] * log_g[:, None])
    s_dec = jnp.exp(chunk * log_g)
    bc = lambda a: jnp.broadcast_to(a[:, :, None], (N_HEADS, chunk, HEAD_DIM))
    return intra, bc(q_dec), bc(k_dec), jnp.broadcast_to(s_dec[:, None, None], (N_HEADS, 1, HEAD_DIM))


def _retention_prompt(rq, rk, rv, gn, batch, seq, tr):
    chunk = RET_CHUNK
    assert seq % tr == 0 and tr % chunk == 0
    nc = seq // tr
    intra, qdec, kdec, sdec = _ret_tables(chunk)
    row = pl.BlockSpec((tr, RET_W), lambda b, c: (b * nc + c, 0))
    full = lambda a: pl.BlockSpec(a.shape, lambda b, c: (0,) * a.ndim)
    return pl.pallas_call(
        _ret_kernel,
        grid=(batch, nc),
        in_specs=[row, row, row, full(intra), full(qdec), full(kdec), full(sdec), full(gn)],
        out_specs=[row, pl.BlockSpec((None, N_HEADS, HEAD_DIM, HEAD_DIM), lambda b, c: (b, 0, 0, 0))],
        out_shape=[jax.ShapeDtypeStruct((batch * seq, RET_W), BF16),
                   jax.ShapeDtypeStruct((batch, N_HEADS, HEAD_DIM, HEAD_DIM), F32)],
        scratch_shapes=[pltpu.VMEM((N_HEADS, HEAD_DIM, HEAD_DIM), F32)],
        compiler_params=_cparams("parallel", "arbitrary"),
        name="ret_prompt",
    )(rq, rk, rv, intra, qdec, kdec, sdec, gn)


def _mla_kernel(qa_ref, kv_ref, vt_ref, wuv_ref, mla_o, q_scr, m_scr, l_scr, acc_scr, *, tq):
    i = pl.program_id(1)
    cols = N_HEADS * tq
    c2 = MLA_SCALE * LOG2E
    for h in range(N_HEADS):
        q_scr[h * tq:(h + 1) * tq, :] = qa_ref[:, h * QK_W:(h + 1) * QK_W]
    def block(blk, off, nk, masked, q_lo=0, first=False):
        kj = kv_ref[pl.ds(pl.multiple_of(blk * tq + off, MLA_SUB), nk), :]
        nq = tq - q_lo
        if masked:
            kpos = blk * tq + off + lax.broadcasted_iota(jnp.int32, (nk, nq), 0)
            qpos = i * tq + q_lo + lax.broadcasted_iota(jnp.int32, (nk, nq), 1)
            keep = kpos <= qpos
        scores = [_dot_nt(kj, q_scr[h * tq + q_lo:(h + 1) * tq, :]) for h in range(N_HEADS)]
        for h in range(N_HEADS):
            cs = slice(h * tq + q_lo, (h + 1) * tq)
            s = scores[h]
            if masked:
                s = jnp.where(keep, s, NEG_BIG)
            m_new = jnp.max(s, axis=0, keepdims=True)
            if not first:
                m_old = m_scr[:, cs]
                m_new = jnp.maximum(m_old, m_new)
                alpha = jnp.exp2((m_old - m_new) * c2)
            p = jnp.exp2((s - m_new) * c2)
            l_new = jnp.sum(p, axis=0, keepdims=True)
            pb = p.astype(BF16)
            pv = None
            for u in range(0, nk, tq):
                w = min(tq, nk - u)
                part = _dot(vt_ref[blk + u // tq, :, off:off + w], pb[u:u + w, :])
                pv = part if pv is None else pv + part
            if first:
                l_scr[:, cs] = l_new
                acc_scr[:, cs] = pv
            else:
                l_scr[:, cs] = alpha * l_scr[:, cs] + l_new
                acc_scr[:, cs] = alpha * acc_scr[:, cs] + pv
            m_scr[:, cs] = m_new

    for u in range(tq // MLA_SUB):
        block(i, u * MLA_SUB, MLA_SUB, True, q_lo=u * MLA_SUB, first=(u == 0))

    def body(j, carry):
        block(2 * j, 0, 2 * tq, False)
        return carry

    lax.fori_loop(0, i // 2, body, 0)

    @pl.when(i % 2 == 1)
    def _():
        block(i - 1, 0, tq, False)
    lat_t = acc_scr[...] / l_scr[...]
    for h in range(N_HEADS):
        lat = jnp.transpose(lat_t[:, h * tq:(h + 1) * tq]).astype(BF16)
        mla_o[:, h * MLA_V:(h + 1) * MLA_V] = _dot(lat, wuv_ref[h]).astype(BF16)


def _mla_prompt(qa, kv, vt, wuv, batch, seq, tq):
    nq = seq // tq
    assert vt.shape == (batch, nq, KV_LORA, tq)
    return pl.pallas_call(
        functools.partial(_mla_kernel, tq=tq),
        grid=(batch, nq),
        in_specs=[pl.BlockSpec((tq, N_HEADS * QK_W), lambda b, i: (b * nq + i, 0)),
                  pl.BlockSpec((seq, QK_W), lambda b, i: (b, 0)),
                  pl.BlockSpec((None, nq, KV_LORA, tq), lambda b, i: (b, 0, 0, 0)),
                  pl.BlockSpec(wuv.shape, lambda b, i: (0, 0, 0))],
        out_specs=pl.BlockSpec((tq, N_HEADS * MLA_V), lambda b, i: (b * nq + i, 0)),
        out_shape=jax.ShapeDtypeStruct((batch * seq, N_HEADS * MLA_V), BF16),
        scratch_shapes=[pltpu.VMEM((N_HEADS * tq, QK_W), BF16), pltpu.VMEM((1, N_HEADS * tq), F32),
                        pltpu.VMEM((1, N_HEADS * tq), F32), pltpu.VMEM((KV_LORA, N_HEADS * tq), F32)],
        compiler_params=_cparams("parallel", "arbitrary"),
        name="mla_prompt",
    )(qa, kv, vt, wuv)


def _memkv_kernel(mem_ref, g_ref, w_ref, k_o, v_o, k4_o, v4_o):
    kvp = _dot(_rms(mem_ref[...], g_ref[...]).astype(BF16), w_ref[...])
    k_o[...] = kvp[:, 0:MEM_W]
    v_o[...] = kvp[:, MEM_W:2 * MEM_W]
    tm = mem_ref.shape[0]
    for h in range(N_HEADS):
        k4_o[pl.ds(h, tm, stride=N_HEADS), :] = kvp[:, h * HEAD_DIM:(h + 1) * HEAD_DIM]
        v4_o[pl.ds(h, tm, stride=N_HEADS), :] = kvp[:, MEM_W + h * HEAD_DIM:MEM_W + (h + 1) * HEAD_DIM]


def _mem_kv(mem2, g, w, tm):
    n, d = mem2.shape
    row = lambda wd: pl.BlockSpec((tm, wd), lambda i: (i, 0))
    row4 = pl.BlockSpec((tm * N_HEADS, HEAD_DIM), lambda i: (i, 0))
    full = lambda a: pl.BlockSpec(a.shape, lambda i: (0,) * a.ndim)
    return pl.pallas_call(
        _memkv_kernel,
        grid=(n // tm,),
        in_specs=[row(d), full(g), full(w)],
        out_specs=[row(MEM_W), row(MEM_W), row4, row4],
        out_shape=[jax.ShapeDtypeStruct((n, MEM_W), F32)] * 2
                  + [jax.ShapeDtypeStruct((n * N_HEADS, HEAD_DIM), F32)] * 2,
        compiler_params=_cparams("parallel"),
        name="mem_kv",
    )(mem2, g, w)


def _out_proj(x, cat_scr, wout_ref, gf_ref):
    return _rms(x + _dot(cat_scr[...], wout_ref[...]), gf_ref[...])


def _merge_prompt_kernel(x_ref, ret_ref, mla_ref, mq_ref, sg_ref, mk_ref, mv_ref, wout_ref, gf_ref,
                         y_o, cat_scr):
    cat_scr[:, 0:RET_W] = ret_ref[...] * sg_ref[:, 0:RET_W]
    cat_scr[:, RET_W:2 * RET_W] = mla_ref[...] * sg_ref[:, RET_W:2 * RET_W]
    heads = [slice(h * HEAD_DIM, (h + 1) * HEAD_DIM) for h in range(N_HEADS)]
    scores = [_dot_nt(mk_ref[:, sl].astype(BF16), mq_ref[:, sl]) for sl in heads]
    part = x_ref[...] + _dot(cat_scr[:, 0:2 * RET_W], wout_ref[0:2 * RET_W, :])
    for h, sl in enumerate(heads):
        s = scores[h] * MEM_SCALE
        p = jnp.exp(s - jnp.max(s, axis=0, keepdims=True))
        p = (p / jnp.sum(p, axis=0, keepdims=True)).astype(BF16)
        o_t = _dot(jnp.transpose(mv_ref[:, sl]).astype(BF16), p)
        gsl = slice(2 * RET_W + h * HEAD_DIM, 2 * RET_W + (h + 1) * HEAD_DIM)
        cat_scr[:, gsl] = (jnp.transpose(o_t) * sg_ref[:, gsl].astype(F32)).astype(BF16)
    half = x_ref.shape[0] // 2
    tails = [_dot(cat_scr[r * half:(r + 1) * half, 2 * RET_W:D_MIX], wout_ref[2 * RET_W:D_MIX, :]) for r in range(2)]
    for r in range(2):
        rows = slice(r * half, (r + 1) * half)
        y_o[rows, :] = _rms(part[rows, :] + tails[r], gf_ref[...])


def _merge_prompt(x2, ret_n, mla, mq, sg, mk, mv, wout, gf, batch, seq, tm):
    nt = seq // tm
    n_mem = mk.shape[0] // batch
    row = lambda w: pl.BlockSpec((tm, w), lambda b, t: (b * nt + t, 0))
    mem = pl.BlockSpec((n_mem, MEM_W), lambda b, t: (b, 0))
    full = lambda a: pl.BlockSpec(a.shape, lambda b, t: (0,) * a.ndim)
    return pl.pallas_call(
        _merge_prompt_kernel,
        grid=(batch, nt),
        in_specs=[row(x2.shape[1]), row(RET_W), row(RET_W), row(MEM_W), row(D_MIX), mem, mem, full(wout), full(gf)],
        out_specs=row(x2.shape[1]),
        out_shape=jax.ShapeDtypeStruct(x2.shape, F32),
        scratch_shapes=[pltpu.VMEM((tm, D_MIX), BF16)],
        compiler_params=_cparams("parallel", "arbitrary"),
        name="merge_prompt",
    )(x2, ret_n, mla, mq, sg, mk, mv, wout, gf)


def _merge_sample_kernel(x_ref, ret_ref, lat_ref, memo_ref, sg_ref, wuv_ref, wout_ref, gf_ref, y_o, cat_scr):
    cat_scr[:, 0:RET_W] = ret_ref[...] * sg_ref[:, 0:RET_W]
    for h in range(N_HEADS):
        lat = lat_ref[:, h * KV_LORA:(h + 1) * KV_LORA].astype(BF16)
        gsl = slice(RET_W + h * MLA_V, RET_W + (h + 1) * MLA_V)
        cat_scr[:, gsl] = (_dot(lat, wuv_ref[h]) * sg_ref[:, gsl].astype(F32)).astype(BF16)
    cat_scr[:, 2 * RET_W:D_MIX] = (memo_ref[...] * sg_ref[:, 2 * RET_W:D_MIX].astype(F32)).astype(BF16)
    y_o[...] = _out_proj(x_ref[...], cat_scr, wout_ref, gf_ref)


def _merge_sample(x2, ret_n, lat, memo, sg, wuv, wout, gf):
    args = (x2, ret_n, lat, memo, sg, wuv, wout, gf)
    return pl.pallas_call(
        _merge_sample_kernel,
        grid=(1,),
        in_specs=[pl.BlockSpec(a.shape, lambda i, nd=a.ndim: (0,) * nd) for a in args],
        out_specs=pl.BlockSpec(x2.shape, lambda i: (0, 0)),
        out_shape=jax.ShapeDtypeStruct(x2.shape, F32),
        scratch_shapes=[pltpu.VMEM((x2.shape[0], D_MIX), BF16)],
        compiler_params=_cparams("arbitrary"),
        name="merge_sample",
    )(*args)


def _ret_step_kernel(rq_ref, rk_ref, rv_ref, s_ref, gam_ref, gn_ref, ret_o, s_o, *, bt):
    sq = (HEAD_DIM, HEAD_DIM)
    for b in range(bt):
        for h in range(N_HEADS):
            sl = slice(h * HEAD_DIM, (h + 1) * HEAD_DIM)
            q = rq_ref[b:b + 1, sl].astype(F32)
            k = rk_ref[b:b + 1, sl].astype(F32)
            v = rv_ref[b:b + 1, sl].astype(F32)
            gam = gam_ref[h]
            s_old = s_ref[b, h]
            q_col = jnp.transpose(jnp.broadcast_to(q, sq))
            k_col = jnp.transpose(jnp.broadcast_to(k, sq))
            qk = jnp.sum(q * k, axis=-1, keepdims=True)
            o = qk * v + gam * jnp.sum(q_col * s_old, axis=0, keepdims=True)
            s_o[b, h] = s_old * gam + k_col * v
            mu = jnp.mean(o, axis=-1, keepdims=True)
            d = o - mu
            var = jnp.mean(d * d, axis=-1, keepdims=True)
            ret_o[b:b + 1, sl] = (d * lax.rsqrt(var + EPS) * gn_ref[:, sl]).astype(BF16)


def _retention_step(rq, rk, rv, state, gn, bt):
    n = rq.shape[0]
    log_g = jnp.log1p(-jnp.exp2(-5.0 - jnp.arange(N_HEADS, dtype=F32)))
    gam = jnp.broadcast_to(jnp.exp(log_g)[:, None, None], (N_HEADS, 1, HEAD_DIM))
    row = pl.BlockSpec((bt, RET_W), lambda i: (i, 0))
    st = pl.BlockSpec((bt, N_HEADS, HEAD_DIM, HEAD_DIM), lambda i: (i, 0, 0, 0))
    full = lambda a: pl.BlockSpec(a.shape, lambda i: (0,) * a.ndim)
    return pl.pallas_call(
        functools.partial(_ret_step_kernel, bt=bt),
        grid=(n // bt,),
        in_specs=[row, row, row, st, full(gam), full(gn)],
        out_specs=[row, st],
        out_shape=[jax.ShapeDtypeStruct((n, RET_W), BF16), jax.ShapeDtypeStruct(state.shape, F32)],
        compiler_params=_cparams("parallel"),
        name="ret_step",
    )(rq, rk, rv, state, gam, gn)


def _mla_dec_kernel(pt_ref, q_ref, kvn_ref, ckv_hbm, kpet_hbm, o_ref,
                    ckv_buf, kpe_buf, sems, m_scr, l_scr, acc_scr, *, cp, page, n_sub):
    b, j = pl.program_id(0), pl.program_id(1)
    nb, nj = pl.num_programs(0), pl.num_programs(1)
    total = nb * nj
    t = b * nj + j
    slot = t % N_SLOTS
    last = t == total - 1
    ahead = N_SLOTS - 1

    def coords(tt):
        ok = tt < total
        return jnp.where(ok, tt // nj, b), jnp.where(ok, tt % nj, j)

    b_next, j_next = coords(t + ahead)
    slot_next = (t + ahead) % N_SLOTS

    def page_copy(which, bb, jj, sl, r, lookup=True):
        pg = pt_ref[bb, jj * cp + r] if lookup else 0
        if which == 0:
            return pltpu.make_async_copy(ckv_hbm.at[pg], ckv_buf.at[sl, r], sems.at[0, sl])
        return pltpu.make_async_copy(kpet_hbm.at[pg], kpe_buf.at[sl, :, r * page:(r + 1) * page], sems.at[1, sl])

    def page_copies(bb, jj, sl, r, lookup=True):
        return tuple(page_copy(w, bb, jj, sl, r, lookup) for w in (0, 1))

    @pl.when(t == 0)
    def _():
        for tt in range(ahead):
            bb, jj = coords(tt)
            for r in range(cp):
                for cpy in page_copies(bb, jj, tt, r):
                    cpy.start()

    @pl.when(j == 0)
    def _():
        m_scr[...] = jnp.full_like(m_scr, NEG_BIG)
        l_scr[...] = jnp.zeros_like(l_scr)
        acc_scr[...] = jnp.zeros_like(acc_scr)

    q = q_ref[...]
    ql, qp = q[:, 0:KV_LORA], q[:, KV_LORA:KV_LORA + MLA_ROPE]
    c2 = MLA_SCALE * LOG2E

    def update(s_parts, v_loaders, between=None):
        m_old = m_scr[...]
        m_new = m_old
        for s in s_parts:
            m_new = jnp.maximum(m_new, jnp.max(s, axis=-1, keepdims=True))
        alpha = jnp.exp2((m_old - m_new) * c2)
        l = alpha * l_scr[...]
        acc = alpha * acc_scr[...]
        for u, (s, load_v) in enumerate(zip(s_parts, v_loaders)):
            p = jnp.exp2((s - m_new) * c2)
            l = l + jnp.sum(p, axis=-1, keepdims=True)
            acc = acc + _dot(p.astype(BF16), load_v())
            if between is not None:
                between(u)
        m_scr[...] = m_new
        l_scr[...] = l
        acc_scr[...] = acc

    for r in range(cp):
        for cpy in page_copies(b, j, slot, r, lookup=False):
            cpy.wait()
    per = cp // n_sub

    def load_keys(u):
        return ckv_buf[slot, u * per:(u + 1) * per].reshape(per * page, KV_LORA).astype(BF16)

    def start_next(u, which):
        for r in range(u * per, (u + 1) * per):
            page_copy(which, b_next, j_next, slot_next, r).start()

    s_parts = []
    for u in range(n_sub):
        ks = slice(u * per * page, (u + 1) * per * page)
        s_parts.append(_dot_nt(ql, load_keys(u)) + _dot(qp, kpe_buf[slot, :, ks].astype(BF16)))
        start_next(u, 0)
    update(s_parts, [functools.partial(load_keys, u) for u in range(n_sub)],
           between=lambda u: start_next(u, 1))

    @pl.when(j == nj - 1)
    def _():
        kvn = kvn_ref[...].astype(BF16)
        s = _dot_nt(ql, kvn[:, 0:KV_LORA]) + _dot_nt(qp, kvn[:, KV_LORA:KV_LORA + MLA_ROPE])
        s = jnp.where(lax.broadcasted_iota(jnp.int32, s.shape, 1) == 0, s, NEG_BIG)
        update([s], [lambda: kvn[:, 0:KV_LORA]])
        o_ref[...] = acc_scr[...] / l_scr[...]

    @pl.when(last)
    def _():
        for k in range(1, N_SLOTS):
            for r in range(cp):
                for cpy in page_copies(b, j, (t + k) % N_SLOTS, r, lookup=False):
                    cpy.wait()


def _mla_decode(q8, kvn8, pool_ckv, pool_kpet, page_table, cp):
    n, n_pages = page_table.shape
    page = pool_ckv.shape[1]
    qw = q8.shape[-1]
    spec_q = pl.BlockSpec((None, 8, qw), lambda b, j, pt: (b, 0, 0))
    grid_spec = pltpu.PrefetchScalarGridSpec(
        num_scalar_prefetch=1,
        grid=(n, n_pages // cp),
        in_specs=[spec_q, spec_q, pl.BlockSpec(memory_space=pl.ANY), pl.BlockSpec(memory_space=pl.ANY)],
        out_specs=pl.BlockSpec((None, 8, KV_LORA), lambda b, j, pt: (b, 0, 0)),
        scratch_shapes=[pltpu.VMEM((N_SLOTS, cp, page, KV_LORA), F32),
                        pltpu.VMEM((N_SLOTS, MLA_ROPE, cp * page), F32),
                        pltpu.SemaphoreType.DMA((2, N_SLOTS)),
                        pltpu.VMEM((8, 1), F32), pltpu.VMEM((8, 1), F32), pltpu.VMEM((8, KV_LORA), F32)],
    )
    return pl.pallas_call(
        functools.partial(_mla_dec_kernel, cp=cp, page=page, n_sub=1),
        grid_spec=grid_spec,
        out_shape=jax.ShapeDtypeStruct((n, 8, KV_LORA), F32),
        compiler_params=_cparams("arbitrary", "arbitrary"),
        name="mla_decode",
    )(page_table, q8, kvn8, pool_ckv, pool_kpet)


def _mem_dec_kernel(q_ref, mk_ref, mv_ref, o_ref, *, bt):
    n_col = mk_ref.shape[1]
    col_head = jnp.bitwise_and(lax.broadcasted_iota(jnp.int32, (8, n_col), 1), N_HEADS - 1)
    own = col_head == lax.broadcasted_iota(jnp.int32, (8, n_col), 0)
    pad = jnp.zeros((8 - N_HEADS, HEAD_DIM), BF16)
    for b in range(bt):
        q8 = jnp.concatenate([q_ref[b], pad], axis=0)
        s = jnp.where(own, _dot_nt(q8, mk_ref[b].astype(BF16)) * MEM_SCALE, NEG_BIG)
        p = jnp.exp(s - jnp.max(s, axis=-1, keepdims=True))
        p = (p / jnp.sum(p, axis=-1, keepdims=True)).astype(BF16)
        o_ref[b] = _dot(p, mv_ref[b].astype(BF16))[0:N_HEADS, :]


def _mem_decode(mq, mk, mv, bt):
    n, n_col, _ = mk.shape
    row = pl.BlockSpec((bt, N_HEADS, HEAD_DIM), lambda i: (i, 0, 0))
    mem = pl.BlockSpec((bt, n_col, HEAD_DIM), lambda i: (i, 0, 0))
    return pl.pallas_call(
        functools.partial(_mem_dec_kernel, bt=bt),
        grid=(n // bt,),
        in_specs=[row, mem, mem],
        out_specs=row,
        out_shape=jax.ShapeDtypeStruct((n, N_HEADS, HEAD_DIM), F32),
        compiler_params=_cparams("parallel"),
        name="mem_decode",
    )(mq, mk, mv)


def _rope_tables(pos, n_freq, slot):
    inv = ROPE_BASE ** (-jnp.arange(0, 2 * n_freq, 2, dtype=F32) / (2 * n_freq))
    ang = pos.astype(F32)[:, None] * inv[None, :]
    c, s = jnp.cos(ang), jnp.sin(ang)
    z = jnp.zeros((pos.shape[0], slot // 2 - n_freq), F32)
    return jnp.concatenate([c, z, c, z], axis=1), jnp.concatenate([-s, z, s, z], axis=1)


def _spread(w):
    z = jnp.zeros(w.shape[:-1] + (32,), w.dtype)
    return jnp.concatenate([w[..., 0:32], z, w[..., 32:64], z], axis=-1)


def _prep_weights(w_in, w_uq, w_uk, w_uv, w_mem_kv, w_out):
    o = [0, 512, 1024, 1536, 1920, 2176, 2240, 2752, 4288]
    seg = lambda i: w_in[:, o[i]:o[i + 1]]
    win = jnp.concatenate([seg(0), seg(1), seg(2), seg(3), _spread(seg(5)), seg(4), seg(6), seg(7)], axis=1)
    wuq = jnp.concatenate([w_uq[:, :, :MLA_NOPE].reshape(Q_LORA, -1),
                           _spread(w_uq[:, :, MLA_NOPE:]).reshape(Q_LORA, -1)], axis=1)
    wuk = jnp.transpose(w_uk, (1, 2, 0))
    wuv = jnp.transpose(w_uv, (1, 0, 2))
    return tuple(a.astype(BF16) for a in (win, wuq, wuk, wuv, w_mem_kv, w_out))


def kernel(x_prompt, x_sample, mem_prompt, cache_ckv, cache_kpe, page_table, state_ret, cache_mem_k, cache_mem_v,
           norm_g, w_in, ret_gn_g, mla_qnorm_g, w_uq, w_uk, mla_kvnorm_g, w_uv, mem_norm_g, w_mem_kv, w_out,
           final_norm_g):
    batch, seq, d_model = x_prompt.shape
    n_dec = x_sample.shape[0]
    n_mem = mem_prompt.shape[1]
    depth = w_in.shape[0]
    assert depth == 1 and x_sample.shape[1] == 1
    l = 0
    win, wuq, wuk, wuv, wmem, wout = _prep_weights(w_in[l], w_uq[l], w_uk[l], w_uv[l], w_mem_kv[l], w_out[l])
    g_in = norm_g[l][None, :]
    g_q = mla_qnorm_g[l][None, :]
    g_kv = mla_kvnorm_g[l][None, :]
    g_gn = ret_gn_g[l][None, :]
    g_mem = mem_norm_g[l][None, :]
    g_fin = final_norm_g[None, :]

    tm = min(512, seq)
    tq = min(512, seq)
    n_pages = page_table.shape[1]
    cp = min(64, n_pages)

    xp = x_prompt.reshape(batch * seq, d_model)
    pos_p = jnp.arange(seq, dtype=jnp.int32)
    tabs_p = _rope_tables(pos_p, HEAD_DIM // 2, LANES) + _rope_tables(pos_p, MLA_ROPE // 2, LANES)
    rq, rk, rv, qa, kvb, ckv, mq, sg, kpet, vt = _project(
        xp, tabs_p, g_in, win, g_q, wuq, wuk, g_kv, batch, seq, tm, tq)
    ret_n, ret_state_p = _retention_prompt(rq, rk, rv, g_gn, batch, seq, min(2048, seq))
    mla = _mla_prompt(qa, kvb, vt, wuv, batch, seq, tq)
    mk, mv, mk4, mv4 = _mem_kv(mem_prompt.reshape(batch * n_mem, d_model), g_mem, wmem, min(512, batch * n_mem))
    y_p = _merge_prompt(xp, ret_n, mla, mq, sg, mk, mv, wout, g_fin, batch, seq, tm)

    xs = x_sample.reshape(n_dec, d_model)
    pos_s = jnp.full((n_dec,), PAST_LEN, dtype=jnp.int32)
    tabs_s = _rope_tables(pos_s, HEAD_DIM // 2, LANES) + _rope_tables(pos_s, MLA_ROPE // 2, LANES)
    rq_s, rk_s, rv_s, qa_s, _, ckv_s, mq_s, sg_s, kpet_s, _ = _project(
        xs, tabs_s, g_in, win, g_q, wuq, wuk, g_kv, 1, n_dec, n_dec, n_dec)
    kpe_s = jnp.swapaxes(kpet_s[0], 0, 1)
    ret_n_s, ret_state_s = _retention_step(rq_s, rk_s, rv_s, state_ret[l], g_gn, 8)
    qh = qa_s.reshape(n_dec, N_HEADS, QK_W)
    q_std = jnp.concatenate([qh[..., 0:KV_LORA], qh[..., KV_LORA:KV_LORA + 32], qh[..., KV_LORA + 64:KV_LORA + 96]], -1)
    q8 = jnp.pad(q_std, ((0, 0), (0, 8 - N_HEADS), (0, 0)))
    kvn8 = jnp.pad(jnp.concatenate([ckv_s, kpe_s], axis=-1)[:, None, :], ((0, 0), (0, 7), (0, 0)))
    lat_s = _mla_decode(q8, kvn8, cache_ckv[l], jnp.swapaxes(cache_kpe[l], 1, 2), page_table, cp)
    lat_s = lat_s[:, 0:N_HEADS, :].reshape(n_dec, N_HEADS * KV_LORA)
    memo_s = _mem_decode(mq_s.reshape(n_dec, N_HEADS, HEAD_DIM),
                         cache_mem_k[l].reshape(n_dec, n_mem * N_HEADS, HEAD_DIM),
                         cache_mem_v[l].reshape(n_dec, n_mem * N_HEADS, HEAD_DIM), 8).reshape(n_dec, MEM_W)
    y_s = _merge_sample(xs, ret_n_s, lat_s, memo_s, sg_s, wuv, wout, g_fin)

    return (y_p.reshape(batch, seq, d_model), y_s.reshape(n_dec, 1, d_model),
            ckv.reshape(1, batch, seq, KV_LORA), jnp.swapaxes(kpet, 1, 2)[None],
            ret_state_p[None], mk4.reshape(1, batch, n_mem, N_HEADS, HEAD_DIM), mv4.reshape(1, batch, n_mem, N_HEADS, HEAD_DIM),
            ckv_s.reshape(1, n_dec, 1, KV_LORA), kpe_s.reshape(1, n_dec, 1, MLA_ROPE), ret_state_s[None])
```

```python
import functools

import jax
import jax.numpy as jnp
from jax import lax
from jax.experimental import pallas as pl
from jax.experimental.pallas import tpu as pltpu

F32 = jnp.float32
BF16 = jnp.bfloat16

HEAD_DIM = 128
N_HEADS = 4
RET_W = N_HEADS * HEAD_DIM
MLA_NOPE = 128
MLA_ROPE = 64
MLA_V = 128
Q_LORA = 384
KV_LORA = 256
MEM_W = N_HEADS * HEAD_DIM
D_MIX = 3 * RET_W
RET_CHUNK = 256
PAST_LEN = 16384
ROPE_BASE = 10000.0
EPS = 1e-6
MLA_SCALE = (MLA_NOPE + MLA_ROPE) ** -0.5
MEM_SCALE = HEAD_DIM ** -0.5
RK_SCALE = HEAD_DIM ** -0.5
NEG_BIG = -1e30
LOG2E = 1.4426950408889634
N_SLOTS = 3
MLA_SUB = 256

LANES = 128
QK_W = KV_LORA + LANES
OFF_RQ, OFF_RK, OFF_RV = 0, RET_W, 2 * RET_W
OFF_CQ = 3 * RET_W
OFF_KPE = OFF_CQ + Q_LORA
OFF_CKV = OFF_KPE + LANES
OFF_MQ = OFF_CKV + KV_LORA
OFF_GATE = OFF_MQ + MEM_W
D_IN2 = OFF_GATE + D_MIX

VMEM_LIMIT = 48 * 1024 * 1024


def _cparams(*sem):
    return pltpu.CompilerParams(dimension_semantics=sem, vmem_limit_bytes=VMEM_LIMIT)


def _rms(x, g):
    return x * lax.rsqrt(jnp.mean(x * x, axis=-1, keepdims=True) + EPS) * g


def _dot(a, b):
    return jnp.dot(a, b, preferred_element_type=F32)


def _dot_nt(a, b):
    return lax.dot_general(a, b, (((1,), (1,)), ((), ())), preferred_element_type=F32)


def _rot_half(x, cos, sin):
    return x * cos + pltpu.roll(x, 64, 1) * sin


def _proj_kernel(x_ref, cr_ref, sr_ref, cm_ref, sm_ref, g_ref, win_ref, gq_ref, wuq_ref, wuk_ref, gkv_ref,
                 rq_o, rk_o, rv_o, qa_o, kv_o, ckv_o, mq_o, sg_o, kpet_o, vt_o):
    xn = _rms(x_ref[...], g_ref[...]).astype(BF16)
    cr, sr = cr_ref[...], sr_ref[...]
    cm, sm = cm_ref[...], sm_ref[...]

    seg = lambda off, w: _dot(xn, win_ref[:, off:off + w])
    zck = seg(OFF_CQ, Q_LORA + LANES)
    zcq, zkpe = zck[:, 0:Q_LORA], zck[:, Q_LORA:Q_LORA + LANES]
    gate = seg(OFF_GATE, D_MIX)
    cq = _rms(zcq, gq_ref[...]).astype(BF16)
    zq, zk = seg(OFF_RQ, RET_W), seg(OFF_RK, RET_W)
    q = _dot(cq, wuq_ref[...])
    sg_o[...] = (gate / (1.0 + jnp.exp(-gate))).astype(BF16)
    zckv = seg(OFF_CKV, KV_LORA)
    for h in range(N_HEADS):
        qn = q[:, h * MLA_NOPE:(h + 1) * MLA_NOPE].astype(BF16)
        qa_o[:, h * QK_W:h * QK_W + KV_LORA] = _dot(qn, wuk_ref[h]).astype(BF16)
        qp = q[:, RET_W + h * LANES:RET_W + (h + 1) * LANES]
        qa_o[:, h * QK_W + KV_LORA:(h + 1) * QK_W] = _rot_half(qp, cm, sm).astype(BF16)
    zv, zm = seg(OFF_RV, RET_W), seg(OFF_MQ, MEM_W)
    for h in range(N_HEADS):
        sl = slice(h * HEAD_DIM, (h + 1) * HEAD_DIM)
        rq_o[:, sl] = _rot_half(zq[:, sl], cr, sr).astype(BF16)
        rk_o[:, sl] = (_rot_half(zk[:, sl], cr, sr) * RK_SCALE).astype(BF16)

    ckv = _rms(zckv, gkv_ref[...])
    ckv_o[...] = ckv
    kv_o[:, 0:KV_LORA] = ckv.astype(BF16)
    kp = _rot_half(zkpe, cm, sm)
    kv_o[:, KV_LORA:QK_W] = kp.astype(BF16)
    kpt = jnp.transpose(kp)
    kpet_o[0:32, :] = kpt[0:32, :]
    kpet_o[32:64, :] = kpt[64:96, :]
    vt = jnp.transpose(ckv).astype(BF16)
    tkb = vt_o.shape[-1]
    for u in range(vt_o.shape[0]):
        vt_o[u] = vt[:, u * tkb:(u + 1) * tkb]
    rv_o[...] = zv.astype(BF16)
    mq_o[...] = zm.astype(BF16)


def _project(x2, tabs, g, win, gq, wuq, wuk, gkv, batch, seq, tm, tkb):
    n, d = x2.shape
    nt = seq // tm
    row = lambda w: pl.BlockSpec((tm, w), lambda i: (i, 0))
    tab = pl.BlockSpec((tm, LANES), lambda i: (i % nt, 0))
    full = lambda a: pl.BlockSpec(a.shape, lambda i: (0,) * a.ndim)
    outs = [(RET_W, BF16), (RET_W, BF16), (RET_W, BF16), (N_HEADS * QK_W, BF16), (QK_W, BF16),
            (KV_LORA, F32), (MEM_W, BF16), (D_MIX, BF16)]
    kpet_spec = pl.BlockSpec((None, MLA_ROPE, tm), lambda i: (i // nt, 0, i % nt))
    vt_spec = pl.BlockSpec((None, tm // tkb, KV_LORA, tkb), lambda i: (i // nt, i % nt, 0, 0))
    return pl.pallas_call(
        _proj_kernel,
        grid=(n // tm,),
        in_specs=[row(d), tab, tab, tab, tab, full(g), full(win), full(gq), full(wuq), full(wuk), full(gkv)],
        out_specs=[row(w) for w, _ in outs] + [kpet_spec, vt_spec],
        out_shape=[jax.ShapeDtypeStruct((n, w), dt) for w, dt in outs]
                  + [jax.ShapeDtypeStruct((batch, MLA_ROPE, seq), F32),
                     jax.ShapeDtypeStruct((batch, seq // tkb, KV_LORA, tkb), BF16)],
        compiler_params=_cparams("parallel"),
        name="proj",
    )(x2, *tabs, g, win, gq, wuq, wuk, gkv)


def _ret_kernel(rq_ref, rk_ref, rv_ref, intra_ref, qdec_ref, kdec_ref, sdec_ref, gn_ref,
                ret_o, state_o, s_scr):
    c = pl.program_id(1)

    @pl.when(c == 0)
    def _():
        s_scr[...] = jnp.zeros_like(s_scr)

    chunk = intra_ref.shape[1]
    for ci in range(rq_ref.shape[0] // chunk):
        rs = slice(ci * chunk, (ci + 1) * chunk)
        heads = [slice(h * HEAD_DIM, (h + 1) * HEAD_DIM) for h in range(N_HEADS)]
        qk = [_dot_nt(rq_ref[rs, sl], rk_ref[rs, sl]) for sl in heads]
        qs = [_dot(rq_ref[rs, sl], s_scr[h].astype(BF16)) for h, sl in enumerate(heads)]
        for h, sl in enumerate(heads):
            k, v = rk_ref[rs, sl], rv_ref[rs, sl]
            o = _dot((qk[h] * intra_ref[h]).astype(BF16), v) + qdec_ref[h] * qs[h]
            kd_t = jnp.transpose(k.astype(F32) * kdec_ref[h]).astype(BF16)
            s_scr[h] = s_scr[h] * sdec_ref[h] + _dot(kd_t, v)
            mu = jnp.mean(o, axis=-1, keepdims=True)
            d = o - mu
            var = jnp.mean(d * d, axis=-1, keepdims=True)
            ret_o[rs, sl] = (d * lax.rsqrt(var + EPS) * gn_ref[:, sl]).astype(BF16)

    @pl.when(c == pl.num_programs(1) - 1)
    def _():
        state_o[...] = s_scr[...]


def _ret_tables(chunk):
    log_g = jnp.log1p(-jnp.exp2(-5.0 - jnp.arange(N_HEADS, dtype=F32)))
    idx = jnp.arange(chunk, dtype=F32)
    diff = idx[:, None] - idx[None, :]
    intra = jnp.where(diff[None] >= 0, jnp.exp(jnp.maximum(diff, 0.0)[None] * log_g[:, None, None]), 0.0)
    q_dec = jnp.exp((idx[None, :] + 1.0) * log_g[:, None])
    k_dec = jnp.exp((chunk - 1.0 - idx)[None, :] * log_g[:, None])
    s_dec = jnp.exp(chunk * log_g)
    bc = lambda a: jnp.broadcast_to(a[:, :, None], (N_HEADS, chunk, HEAD_DIM))
    return intra, bc(q_dec), bc(k_dec), jnp.broadcast_to(s_dec[:, None, None], (N_HEADS, 1, HEAD_DIM))


def _retention_prompt(rq, rk, rv, gn, batch, seq, tr):
    chunk = RET_CHUNK
    assert seq % tr == 0 and tr % chunk == 0
    nc = seq // tr
    intra, qdec, kdec, sdec = _ret_tables(chunk)
    row = pl.BlockSpec((tr, RET_W), lambda b, c: (b * nc + c, 0))
    full = lambda a: pl.BlockSpec(a.shape, lambda b, c: (0,) * a.ndim)
    return pl.pallas_call(
        _ret_kernel,
        grid=(batch, nc),
        in_specs=[row, row, row, full(intra), full(qdec), full(kdec), full(sdec), full(gn)],
        out_specs=[row, pl.BlockSpec((None, N_HEADS, HEAD_DIM, HEAD_DIM), lambda b, c: (b, 0, 0, 0))],
        out_shape=[jax.ShapeDtypeStruct((batch * seq, RET_W), BF16),
                   jax.ShapeDtypeStruct((batch, N_HEADS, HEAD_DIM, HEAD_DIM), F32)],
        scratch_shapes=[pltpu.VMEM((N_HEADS, HEAD_DIM, HEAD_DIM), F32)],
        compiler_params=_cparams("parallel", "arbitrary"),
        name="ret_prompt",
    )(rq, rk, rv, intra, qdec, kdec, sdec, gn)


def _mla_kernel(qa_ref, kv_ref, vt_ref, wuv_ref, mla_o, q_scr, m_scr, l_scr, acc_scr, *, tq):
    i = pl.program_id(1)
    cols = N_HEADS * tq
    c2 = MLA_SCALE * LOG2E
    for h in range(N_HEADS):
        q_scr[h * tq:(h + 1) * tq, :] = qa_ref[:, h * QK_W:(h + 1) * QK_W]
    def block(blk, off, nk, masked, q_lo=0, first=False):
        kj = kv_ref[pl.ds(pl.multiple_of(blk * tq + off, MLA_SUB), nk), :]
        nq = tq - q_lo
        if masked:
            kpos = blk * tq + off + lax.broadcasted_iota(jnp.int32, (nk, nq), 0)
            qpos = i * tq + q_lo + lax.broadcasted_iota(jnp.int32, (nk, nq), 1)
            keep = kpos <= qpos
        scores = [_dot_nt(kj, q_scr[h * tq + q_lo:(h + 1) * tq, :]) for h in range(N_HEADS)]
        for h in range(N_HEADS):
            cs = slice(h * tq + q_lo, (h + 1) * tq)
            s = scores[h]
            if masked:
                s = jnp.where(keep, s, NEG_BIG)
            m_new = jnp.max(s, axis=0, keepdims=True)
            if not first:
                m_old = m_scr[:, cs]
                m_new = jnp.maximum(m_old, m_new)
                alpha = jnp.exp2((m_old - m_new) * c2)
            p = jnp.exp2((s - m_new) * c2)
            l_new = jnp.sum(p, axis=0, keepdims=True)
            pb = p.astype(BF16)
            pv = None
            for u in range(0, nk, tq):
                w = min(tq, nk - u)
                part = _dot(vt_ref[blk + u // tq, :, off:off + w], pb[u:u + w, :])
                pv = part if pv is None else pv + part
            if first:
                l_scr[:, cs] = l_new
                acc_scr[:, cs] = pv
            else:
                l_scr[:, cs] = alpha * l_scr[:, cs] + l_new
                acc_scr[:, cs] = alpha * acc_scr[:, cs] + pv
            m_scr[:, cs] = m_new

    for u in range(tq // MLA_SUB):
        block(i, u * MLA_SUB, MLA_SUB, True, q_lo=u * MLA_SUB, first=(u == 0))

    def body(j, carry):
        block(2 * j, 0, 2 * tq, False)
        return carry

    lax.fori_loop(0, i // 2, body, 0)

    @pl.when(i % 2 == 1)
    def _():
        block(i - 1, 0, tq, False)
    lat_t = acc_scr[...] / l_scr[...]
    for h in range(N_HEADS):
        lat = jnp.transpose(lat_t[:, h * tq:(h + 1) * tq]).astype(BF16)
        mla_o[:, h * MLA_V:(h + 1) * MLA_V] = _dot(lat, wuv_ref[h]).astype(BF16)


def _mla_prompt(qa, kv, vt, wuv, batch, seq, tq):
    nq = seq // tq
    assert vt.shape == (batch, nq, KV_LORA, tq)
    return pl.pallas_call(
        functools.partial(_mla_kernel, tq=tq),
        grid=(batch, nq),
        in_specs=[pl.BlockSpec((tq, N_HEADS * QK_W), lambda b, i: (b * nq + i, 0)),
                  pl.BlockSpec((seq, QK_W), lambda b, i: (b, 0)),
                  pl.BlockSpec((None, nq, KV_LORA, tq), lambda b, i: (b, 0, 0, 0)),
                  pl.BlockSpec(wuv.shape, lambda b, i: (0, 0, 0))],
        out_specs=pl.BlockSpec((tq, N_HEADS * MLA_V), lambda b, i: (b * nq + i, 0)),
        out_shape=jax.ShapeDtypeStruct((batch * seq, N_HEADS * MLA_V), BF16),
        scratch_shapes=[pltpu.VMEM((N_HEADS * tq, QK_W), BF16), pltpu.VMEM((1, N_HEADS * tq), F32),
                        pltpu.VMEM((1, N_HEADS * tq), F32), pltpu.VMEM((KV_LORA, N_HEADS * tq), F32)],
        compiler_params=_cparams("parallel", "arbitrary"),
        name="mla_prompt",
    )(qa, kv, vt, wuv)


def _memkv_kernel(mem_ref, g_ref, w_ref, k_o, v_o, k4_o, v4_o):
    kvp = _dot(_rms(mem_ref[...], g_ref[...]).astype(BF16), w_ref[...])
    k_o[...] = kvp[:, 0:MEM_W]
    v_o[...] = kvp[:, MEM_W:2 * MEM_W]
    tm = mem_ref.shape[0]
    for h in range(N_HEADS):
        k4_o[pl.ds(h, tm, stride=N_HEADS), :] = kvp[:, h * HEAD_DIM:(h + 1) * HEAD_DIM]
        v4_o[pl.ds(h, tm, stride=N_HEADS), :] = kvp[:, MEM_W + h * HEAD_DIM:MEM_W + (h + 1) * HEAD_DIM]


def _mem_kv(mem2, g, w, tm):
    n, d = mem2.shape
    row = lambda wd: pl.BlockSpec((tm, wd), lambda i: (i, 0))
    row4 = pl.BlockSpec((tm * N_HEADS, HEAD_DIM), lambda i: (i, 0))
    full = lambda a: pl.BlockSpec(a.shape, lambda i: (0,) * a.ndim)
    return pl.pallas_call(
        _memkv_kernel,
        grid=(n // tm,),
        in_specs=[row(d), full(g), full(w)],
        out_specs=[row(MEM_W), row(MEM_W), row4, row4],
        out_shape=[jax.ShapeDtypeStruct((n, MEM_W), F32)] * 2
                  + [jax.ShapeDtypeStruct((n * N_HEADS, HEAD_DIM), F32)] * 2,
        compiler_params=_cparams("parallel"),
        name="mem_kv",
    )(mem2, g, w)


def _out_proj(x, cat_scr, wout_ref, gf_ref):
    return _rms(x + _dot(cat_scr[...], wout_ref[...]), gf_ref[...])


def _merge_prompt_kernel(x_ref, ret_ref, mla_ref, mq_ref, sg_ref, mk_ref, mv_ref, wout_ref, gf_ref,
                         y_o, cat_scr):
    cat_scr[:, 0:RET_W] = ret_ref[...] * sg_ref[:, 0:RET_W]
    cat_scr[:, RET_W:2 * RET_W] = mla_ref[...] * sg_ref[:, RET_W:2 * RET_W]
    heads = [slice(h * HEAD_DIM, (h + 1) * HEAD_DIM) for h in range(N_HEADS)]
    scores = [_dot_nt(mk_ref[:, sl].astype(BF16), mq_ref[:, sl]) for sl in heads]
    part = x_ref[...] + _dot(cat_scr[:, 0:2 * RET_W], wout_ref[0:2 * RET_W, :])
    for h, sl in enumerate(heads):
        s = scores[h] * MEM_SCALE
        p = jnp.exp(s - jnp.max(s, axis=0, keepdims=True))
        p = (p / jnp.sum(p, axis=0, keepdims=True)).astype(BF16)
        o_t = _dot(jnp.transpose(mv_ref[:, sl]).astype(BF16), p)
        gsl = slice(2 * RET_W + h * HEAD_DIM, 2 * RET_W + (h + 1) * HEAD_DIM)
        cat_scr[:, gsl] = (jnp.transpose(o_t) * sg_ref[:, gsl].astype(F32)).astype(BF16)
    half = x_ref.shape[0] // 2
    tails = [_dot(cat_scr[r * half:(r + 1) * half, 2 * RET_W:D_MIX], wout_ref[2 * RET_W:D_MIX, :]) for r in range(2)]
    for r in range(2):
        rows = slice(r * half, (r + 1) * half)
        y_o[rows, :] = _rms(part[rows, :] + tails[r], gf_ref[...])


def _merge_prompt(x2, ret_n, mla, mq, sg, mk, mv, wout, gf, batch, seq, tm):
    nt = seq // tm
    n_mem = mk.shape[0] // batch
    row = lambda w: pl.BlockSpec((tm, w), lambda b, t: (b * nt + t, 0))
    mem = pl.BlockSpec((n_mem, MEM_W), lambda b, t: (b, 0))
    full = lambda a: pl.BlockSpec(a.shape, lambda b, t: (0,) * a.ndim)
    return pl.pallas_call(
        _merge_prompt_kernel,
        grid=(batch, nt),
        in_specs=[row(x2.shape[1]), row(RET_W), row(RET_W), row(MEM_W), row(D_MIX), mem, mem, full(wout), full(gf)],
        out_specs=row(x2.shape[1]),
        out_shape=jax.ShapeDtypeStruct(x2.shape, F32),
        scratch_shapes=[pltpu.VMEM((tm, D_MIX), BF16)],
        compiler_params=_cparams("parallel", "arbitrary"),
        name="merge_prompt",
    )(x2, ret_n, mla, mq, sg, mk, mv, wout, gf)


def _merge_sample_kernel(x_ref, ret_ref, lat_ref, memo_ref, sg_ref, wuv_ref, wout_ref, gf_ref, y_o, cat_scr):
    cat_scr[:, 0:RET_W] = ret_ref[...] * sg_ref[:, 0:RET_W]
    for h in range(N_HEADS):
        lat = lat_ref[:, h * KV_LORA:(h + 1) * KV_LORA].astype(BF16)
        gsl = slice(RET_W + h * MLA_V, RET_W + (h + 1) * MLA_V)
        cat_scr[:, gsl] = (_dot(lat, wuv_ref[h]) * sg_ref[:, gsl].astype(F32)).astype(BF16)
    cat_scr[:, 2 * RET_W:D_MIX] = (memo_ref[...] * sg_ref[:, 2 * RET_W:D_MIX].astype(F32)).astype(BF16)
    y_o[...] = _out_proj(x_ref[...], cat_scr, wout_ref, gf_ref)


def _merge_sample(x2, ret_n, lat, memo, sg, wuv, wout, gf):
    args = (x2, ret_n, lat, memo, sg, wuv, wout, gf)
    return pl.pallas_call(
        _merge_sample_kernel,
        grid=(1,),
        in_specs=[pl.BlockSpec(a.shape, lambda i, nd=a.ndim: (0,) * nd) for a in args],
        out_specs=pl.BlockSpec(x2.shape, lambda i: (0, 0)),
        out_shape=jax.ShapeDtypeStruct(x2.shape, F32),
        scratch_shapes=[pltpu.VMEM((x2.shape[0], D_MIX), BF16)],
        compiler_params=_cparams("arbitrary"),
        name="merge_sample",
    )(*args)


def _ret_step_kernel(rq_ref, rk_ref, rv_ref, s_ref, gam_ref, gn_ref, ret_o, s_o, *, bt):
    sq = (HEAD_DIM, HEAD_DIM)
    heads = [slice(h * HEAD_DIM, (h + 1) * HEAD_DIM) for h in range(N_HEADS)]
    pairs = [(b, h) for b in range(bt) for h in range(N_HEADS)]
    qs = {(b, h): rq_ref[b:b + 1, heads[h]].astype(F32) for b, h in pairs}
    ks = {(b, h): rk_ref[b:b + 1, heads[h]].astype(F32) for b, h in pairs}
    q_cols = {bh: jnp.transpose(jnp.broadcast_to(qs[bh], sq)) for bh in pairs}
    k_cols = {bh: jnp.transpose(jnp.broadcast_to(ks[bh], sq)) for bh in pairs}
    for b, h in pairs:
        sl = heads[h]
        q, k = qs[b, h], ks[b, h]
        v = rv_ref[b:b + 1, sl].astype(F32)
        gam = gam_ref[h]
        s_old = s_ref[b, h]
        qk = jnp.sum(q * k, axis=-1, keepdims=True)
        o = qk * v + gam * jnp.sum(q_cols[b, h] * s_old, axis=0, keepdims=True)
        s_o[b, h] = s_old * gam + k_cols[b, h] * v
        mu = jnp.mean(o, axis=-1, keepdims=True)
        d = o - mu
        var = jnp.mean(d * d, axis=-1, keepdims=True)
        ret_o[b:b + 1, sl] = (d * lax.rsqrt(var + EPS) * gn_ref[:, sl]).astype(BF16)


def _retention_step(rq, rk, rv, state, gn, bt):
    n = rq.shape[0]
    log_g = jnp.log1p(-jnp.exp2(-5.0 - jnp.arange(N_HEADS, dtype=F32)))
    gam = jnp.broadcast_to(jnp.exp(log_g)[:, None, None], (N_HEADS, 1, HEAD_DIM))
    row = pl.BlockSpec((bt, RET_W), lambda i: (i, 0))
    st = pl.BlockSpec((bt, N_HEADS, HEAD_DIM, HEAD_DIM), lambda i: (i, 0, 0, 0))
    full = lambda a: pl.BlockSpec(a.shape, lambda i: (0,) * a.ndim)
    return pl.pallas_call(
        functools.partial(_ret_step_kernel, bt=bt),
        grid=(n // bt,),
        in_specs=[row, row, row, st, full(gam), full(gn)],
        out_specs=[row, st],
        out_shape=[jax.ShapeDtypeStruct((n, RET_W), BF16), jax.ShapeDtypeStruct(state.shape, F32)],
        compiler_params=_cparams("parallel"),
        name="ret_step",
    )(rq, rk, rv, state, gam, gn)


def _mla_dec_kernel(pt_ref, q_ref, kvn_ref, ckv_hbm, kpet_hbm, o_ref,
                    ckv_buf, kpe_buf, sems, m_scr, l_scr, acc_scr, *, cp, page, n_sub):
    b, j = pl.program_id(0), pl.program_id(1)
    nb, nj = pl.num_programs(0), pl.num_programs(1)
    total = nb * nj
    t = b * nj + j
    slot = t % N_SLOTS
    last = t == total - 1
    ahead = N_SLOTS - 1

    def coords(tt):
        ok = tt < total
        return jnp.where(ok, tt // nj, b), jnp.where(ok, tt % nj, j)

    b_next, j_next = coords(t + ahead)
    slot_next = (t + ahead) % N_SLOTS

    def page_copy(which, bb, jj, sl, r, lookup=True):
        pg = pt_ref[bb, jj * cp + r] if lookup else 0
        if which == 0:
            return pltpu.make_async_copy(ckv_hbm.at[pg], ckv_buf.at[sl, r], sems.at[0, sl])
        return pltpu.make_async_copy(kpet_hbm.at[pg], kpe_buf.at[sl, :, r * page:(r + 1) * page], sems.at[1, sl])

    def page_copies(bb, jj, sl, r, lookup=True):
        return tuple(page_copy(w, bb, jj, sl, r, lookup) for w in (0, 1))

    @pl.when(t == 0)
    def _():
        for tt in range(ahead):
            bb, jj = coords(tt)
            for r in range(cp):
                for cpy in page_copies(bb, jj, tt, r):
                    cpy.start()

    @pl.when(j == 0)
    def _():
        m_scr[...] = jnp.full_like(m_scr, NEG_BIG)
        l_scr[...] = jnp.zeros_like(l_scr)
        acc_scr[...] = jnp.zeros_like(acc_scr)

    q = q_ref[...]
    ql, qp = q[:, 0:KV_LORA], q[:, KV_LORA:KV_LORA + MLA_ROPE]
    c2 = MLA_SCALE * LOG2E

    def update(s_parts, v_loaders, between=None):
        m_old = m_scr[...]
        m_new = m_old
        for s in s_parts:
            m_new = jnp.maximum(m_new, jnp.max(s, axis=-1, keepdims=True))
        alpha = jnp.exp2((m_old - m_new) * c2)
        l = alpha * l_scr[...]
        acc = alpha * acc_scr[...]
        for u, (s, load_v) in enumerate(zip(s_parts, v_loaders)):
            p = jnp.exp2((s - m_new) * c2)
            l = l + jnp.sum(p, axis=-1, keepdims=True)
            acc = acc + _dot(p.astype(BF16), load_v())
            if between is not None:
                between(u)
        m_scr[...] = m_new
        l_scr[...] = l
        acc_scr[...] = acc

    for r in range(cp):
        for cpy in page_copies(b, j, slot, r, lookup=False):
            cpy.wait()
    per = cp // n_sub

    def load_keys(u):
        return ckv_buf[slot, u * per:(u + 1) * per].reshape(per * page, KV_LORA).astype(BF16)

    def start_next(u, which):
        for r in range(u * per, (u + 1) * per):
            page_copy(which, b_next, j_next, slot_next, r).start()

    s_parts = []
    for u in range(n_sub):
        ks = slice(u * per * page, (u + 1) * per * page)
        s_parts.append(_dot_nt(ql, load_keys(u)) + _dot(qp, kpe_buf[slot, :, ks].astype(BF16)))
        start_next(u, 0)
    update(s_parts, [functools.partial(load_keys, u) for u in range(n_sub)],
           between=lambda u: start_next(u, 1))

    @pl.when(j == nj - 1)
    def _():
        kvn = kvn_ref[...].astype(BF16)
        s = _dot_nt(ql, kvn[:, 0:KV_LORA]) + _dot_nt(qp, kvn[:, KV_LORA:KV_LORA + MLA_ROPE])
        s = jnp.where(lax.broadcasted_iota(jnp.int32, s.shape, 1) == 0, s, NEG_BIG)
        update([s], [lambda: kvn[:, 0:KV_LORA]])
        o_ref[...] = acc_scr[...] / l_scr[...]

    @pl.when(last)
    def _():
        for k in range(1, N_SLOTS):
            for r in range(cp):
                for cpy in page_copies(b, j, (t + k) % N_SLOTS, r, lookup=False):
                    cpy.wait()


def _mla_decode(q8, kvn8, pool_ckv, pool_kpet, page_table, cp):
    n, n_pages = page_table.shape
    page = pool_ckv.shape[1]
    qw = q8.shape[-1]
    spec_q = pl.BlockSpec((None, 8, qw), lambda b, j, pt: (b, 0, 0))
    grid_spec = pltpu.PrefetchScalarGridSpec(
        num_scalar_prefetch=1,
        grid=(n, n_pages // cp),
        in_specs=[spec_q, spec_q, pl.BlockSpec(memory_space=pl.ANY), pl.BlockSpec(memory_space=pl.ANY)],
        out_specs=pl.BlockSpec((None, 8, KV_LORA), lambda b, j, pt: (b, 0, 0)),
        scratch_shapes=[pltpu.VMEM((N_SLOTS, cp, page, KV_LORA), F32),
                        pltpu.VMEM((N_SLOTS, MLA_ROPE, cp * page), F32),
                        pltpu.SemaphoreType.DMA((2, N_SLOTS)),
                        pltpu.VMEM((8, 1), F32), pltpu.VMEM((8, 1), F32), pltpu.VMEM((8, KV_LORA), F32)],
    )
    return pl.pallas_call(
        functools.partial(_mla_dec_kernel, cp=cp, page=page, n_sub=1),
        grid_spec=grid_spec,
        out_shape=jax.ShapeDtypeStruct((n, 8, KV_LORA), F32),
        compiler_params=_cparams("arbitrary", "arbitrary"),
        name="mla_decode",
    )(page_table, q8, kvn8, pool_ckv, pool_kpet)


def _mem_dec_kernel(q_ref, mk_ref, mv_ref, o_ref, *, bt):
    n_col = mk_ref.shape[1]
    col_head = jnp.bitwise_and(lax.broadcasted_iota(jnp.int32, (8, n_col), 1), N_HEADS - 1)
    own = col_head == lax.broadcasted_iota(jnp.int32, (8, n_col), 0)
    pad = jnp.zeros((8 - N_HEADS, HEAD_DIM), BF16)
    scores = [_dot_nt(jnp.concatenate([q_ref[b], pad], axis=0), mk_ref[b].astype(BF16)) for b in range(bt)]
    probs = []
    for b in range(bt):
        s = jnp.where(own, scores[b] * MEM_SCALE, NEG_BIG)
        p = jnp.exp(s - jnp.max(s, axis=-1, keepdims=True))
        probs.append((p / jnp.sum(p, axis=-1, keepdims=True)).astype(BF16))
    for b in range(bt):
        o_ref[b] = _dot(probs[b], mv_ref[b].astype(BF16))[0:N_HEADS, :]


def _mem_decode(mq, mk, mv, bt):
    n, n_col, _ = mk.shape
    row = pl.BlockSpec((bt, N_HEADS, HEAD_DIM), lambda i: (i, 0, 0))
    mem = pl.BlockSpec((bt, n_col, HEAD_DIM), lambda i: (i, 0, 0))
    return pl.pallas_call(
        functools.partial(_mem_dec_kernel, bt=bt),
        grid=(n // bt,),
        in_specs=[row, mem, mem],
        out_specs=row,
        out_shape=jax.ShapeDtypeStruct((n, N_HEADS, HEAD_DIM), F32),
        compiler_params=_cparams("parallel"),
        name="mem_decode",
    )(mq, mk, mv)


def _rope_tables(pos, n_freq, slot):
    inv = ROPE_BASE ** (-jnp.arange(0, 2 * n_freq, 2, dtype=F32) / (2 * n_freq))
    ang = pos.astype(F32)[:, None] * inv[None, :]
    c, s = jnp.cos(ang), jnp.sin(ang)
    z = jnp.zeros((pos.shape[0], slot // 2 - n_freq), F32)
    return jnp.concatenate([c, z, c, z], axis=1), jnp.concatenate([-s, z, s, z], axis=1)


def _spread(w):
    z = jnp.zeros(w.shape[:-1] + (32,), w.dtype)
    return jnp.concatenate([w[..., 0:32], z, w[..., 32:64], z], axis=-1)


def _prep_weights(w_in, w_uq, w_uk, w_uv, w_mem_kv, w_out):
    o = [0, 512, 1024, 1536, 1920, 2176, 2240, 2752, 4288]
    seg = lambda i: w_in[:, o[i]:o[i + 1]]
    win = jnp.concatenate([seg(0), seg(1), seg(2), seg(3), _spread(seg(5)), seg(4), seg(6), seg(7)], axis=1)
    wuq = jnp.concatenate([w_uq[:, :, :MLA_NOPE].reshape(Q_LORA, -1),
                           _spread(w_uq[:, :, MLA_NOPE:]).reshape(Q_LORA, -1)], axis=1)
    wuk = jnp.transpose(w_uk, (1, 2, 0))
    wuv = jnp.transpose(w_uv, (1, 0, 2))
    return tuple(a.astype(BF16) for a in (win, wuq, wuk, wuv, w_mem_kv, w_out))


def kernel(x_prompt, x_sample, mem_prompt, cache_ckv, cache_kpe, page_table, state_ret, cache_mem_k, cache_mem_v,
           norm_g, w_in, ret_gn_g, mla_qnorm_g, w_uq, w_uk, mla_kvnorm_g, w_uv, mem_norm_g, w_mem_kv, w_out,
           final_norm_g):
    batch, seq, d_model = x_prompt.shape
    n_dec = x_sample.shape[0]
    n_mem = mem_prompt.shape[1]
    depth = w_in.shape[0]
    assert depth == 1 and x_sample.shape[1] == 1
    l = 0
    win, wuq, wuk, wuv, wmem, wout = _prep_weights(w_in[l], w_uq[l], w_uk[l], w_uv[l], w_mem_kv[l], w_out[l])
    g_in = norm_g[l][None, :]
    g_q = mla_qnorm_g[l][None, :]
    g_kv = mla_kvnorm_g[l][None, :]
    g_gn = ret_gn_g[l][None, :]
    g_mem = mem_norm_g[l][None, :]
    g_fin = final_norm_g[None, :]

    tm = min(512, seq)
    tq = min(512, seq)
    n_pages = page_table.shape[1]
    cp = min(64, n_pages)

    xp = x_prompt.reshape(batch * seq, d_model)
    pos_p = jnp.arange(seq, dtype=jnp.int32)
    tabs_p = _rope_tables(pos_p, HEAD_DIM // 2, LANES) + _rope_tables(pos_p, MLA_ROPE // 2, LANES)
    rq, rk, rv, qa, kvb, ckv, mq, sg, kpet, vt = _project(
        xp, tabs_p, g_in, win, g_q, wuq, wuk, g_kv, batch, seq, tm, tq)
    ret_n, ret_state_p = _retention_prompt(rq, rk, rv, g_gn, batch, seq, min(2048, seq))
    mla = _mla_prompt(qa, kvb, vt, wuv, batch, seq, tq)
    mk, mv, mk4, mv4 = _mem_kv(mem_prompt.reshape(batch * n_mem, d_model), g_mem, wmem, min(512, batch * n_mem))
    y_p = _merge_prompt(xp, ret_n, mla, mq, sg, mk, mv, wout, g_fin, batch, seq, tm)

    xs = x_sample.reshape(n_dec, d_model)
    pos_s = jnp.full((n_dec,), PAST_LEN, dtype=jnp.int32)
    tabs_s = _rope_tables(pos_s, HEAD_DIM // 2, LANES) + _rope_tables(pos_s, MLA_ROPE // 2, LANES)
    rq_s, rk_s, rv_s, qa_s, _, ckv_s, mq_s, sg_s, kpet_s, _ = _project(
        xs, tabs_s, g_in, win, g_q, wuq, wuk, g_kv, 1, n_dec, n_dec, n_dec)
    kpe_s = jnp.swapaxes(kpet_s[0], 0, 1)
    ret_n_s, ret_state_s = _retention_step(rq_s, rk_s, rv_s, state_ret[l], g_gn, 8)
    qh = qa_s.reshape(n_dec, N_HEADS, QK_W)
    q_std = jnp.concatenate([qh[..., 0:KV_LORA], qh[..., KV_LORA:KV_LORA + 32], qh[..., KV_LORA + 64:KV_LORA + 96]], -1)
    q8 = jnp.pad(q_std, ((0, 0), (0, 8 - N_HEADS), (0, 0)))
    kvn8 = jnp.pad(jnp.concatenate([ckv_s, kpe_s], axis=-1)[:, None, :], ((0, 0), (0, 7), (0, 0)))
    lat_s = _mla_decode(q8, kvn8, cache_ckv[l], jnp.swapaxes(cache_kpe[l], 1, 2), page_table, cp)
    lat_s = lat_s[:, 0:N_HEADS, :].reshape(n_dec, N_HEADS * KV_LORA)
    memo_s = _mem_decode(mq_s.reshape(n_dec, N_HEADS, HEAD_DIM),
                         cache_mem_k[l].reshape(n_dec, n_mem * N_HEADS, HEAD_DIM),
                         cache_mem_v[l].reshape(n_dec, n_mem * N_HEADS, HEAD_DIM), 8).reshape(n_dec, MEM_W)
    y_s = _merge_sample(xs, ret_n_s, lat_s, memo_s, sg_s, wuv, wout, g_fin)

    return (y_p.reshape(batch, seq, d_model), y_s.reshape(n_dec, 1, d_model),
            ckv.reshape(1, batch, seq, KV_LORA), jnp.swapaxes(kpet, 1, 2)[None],
            ret_state_p[None], mk4.reshape(1, batch, n_mem, N_HEADS, HEAD_DIM), mv4.reshape(1, batch, n_mem, N_HEADS, HEAD_DIM),
            ckv_s.reshape(1, n_dec, 1, KV_LORA), kpe_s.reshape(1, n_dec, 1, MLA_ROPE), ret_state_s[None])
```

```python
import functools

import jax
import jax.numpy as jnp
from jax import lax
from jax.experimental import pallas as pl
from jax.experimental.pallas import tpu as pltpu

F32 = jnp.float32
BF16 = jnp.bfloat16

HEAD_DIM = 128
N_HEADS = 4
RET_W = N_HEADS * HEAD_DIM
MLA_NOPE = 128
MLA_ROPE = 64
MLA_V = 128
Q_LORA = 384
KV_LORA = 256
MEM_W = N_HEADS * HEAD_DIM
D_MIX = 3 * RET_W
RET_CHUNK = 256
PAST_LEN = 16384
ROPE_BASE = 10000.0
EPS = 1e-6
MLA_SCALE = (MLA_NOPE + MLA_ROPE) ** -0.5
MEM_SCALE = HEAD_DIM ** -0.5
RK_SCALE = HEAD_DIM ** -0.5
NEG_BIG = -1e30
LOG2E = 1.4426950408889634
N_SLOTS = 3
MLA_SUB = 256

LANES = 128
QK_W = KV_LORA + LANES
OFF_RQ, OFF_RK, OFF_RV = 0, RET_W, 2 * RET_W
OFF_CQ = 3 * RET_W
OFF_KPE = OFF_CQ + Q_LORA
OFF_CKV = OFF_KPE + LANES
OFF_MQ = OFF_CKV + KV_LORA
OFF_GATE = OFF_MQ + MEM_W
D_IN2 = OFF_GATE + D_MIX

VMEM_LIMIT = 48 * 1024 * 1024


def _cparams(*sem):
    return pltpu.CompilerParams(dimension_semantics=sem, vmem_limit_bytes=VMEM_LIMIT)


def _rms(x, g):
    return x * lax.rsqrt(jnp.mean(x * x, axis=-1, keepdims=True) + EPS) * g


def _dot(a, b):
    return jnp.dot(a, b, preferred_element_type=F32)


def _dot_nt(a, b):
    return lax.dot_general(a, b, (((1,), (1,)), ((), ())), preferred_element_type=F32)


def _rot_half(x, cos, sin):
    return x * cos + pltpu.roll(x, 64, 1) * sin


def _proj_kernel(x_ref, cr_ref, sr_ref, cm_ref, sm_ref, g_ref, win_ref, gq_ref, wuq_ref, wuk_ref, gkv_ref,
                 rq_o, rk_o, rv_o, qa_o, kv_o, ckv_o, mq_o, sg_o, kpet_o, vt_o):
    xn = _rms(x_ref[...], g_ref[...]).astype(BF16)
    cr, sr = cr_ref[...], sr_ref[...]
    cm, sm = cm_ref[...], sm_ref[...]

    seg = lambda off, w: _dot(xn, win_ref[:, off:off + w])
    zck = seg(OFF_CQ, Q_LORA + LANES)
    zcq, zkpe = zck[:, 0:Q_LORA], zck[:, Q_LORA:Q_LORA + LANES]
    gate = seg(OFF_GATE, D_MIX)
    cq = _rms(zcq, gq_ref[...]).astype(BF16)
    zq, zk = seg(OFF_RQ, RET_W), seg(OFF_RK, RET_W)
    q = _dot(cq, wuq_ref[...])
    sg_o[...] = (gate / (1.0 + jnp.exp(-gate))).astype(BF16)
    zckv = seg(OFF_CKV, KV_LORA)
    for h in range(N_HEADS):
        qn = q[:, h * MLA_NOPE:(h + 1) * MLA_NOPE].astype(BF16)
        qa_o[:, h * QK_W:h * QK_W + KV_LORA] = _dot(qn, wuk_ref[h]).astype(BF16)
        qp = q[:, RET_W + h * LANES:RET_W + (h + 1) * LANES]
        qa_o[:, h * QK_W + KV_LORA:(h + 1) * QK_W] = _rot_half(qp, cm, sm).astype(BF16)
    zv, zm = seg(OFF_RV, RET_W), seg(OFF_MQ, MEM_W)
    for h in range(N_HEADS):
        sl = slice(h * HEAD_DIM, (h + 1) * HEAD_DIM)
        rq_o[:, sl] = _rot_half(zq[:, sl], cr, sr).astype(BF16)
        rk_o[:, sl] = (_rot_half(zk[:, sl], cr, sr) * RK_SCALE).astype(BF16)

    ckv = _rms(zckv, gkv_ref[...])
    ckv_o[...] = ckv
    kv_o[:, 0:KV_LORA] = ckv.astype(BF16)
    kp = _rot_half(zkpe, cm, sm)
    kv_o[:, KV_LORA:QK_W] = kp.astype(BF16)
    kpt = jnp.transpose(kp)
    kpet_o[0:32, :] = kpt[0:32, :]
    kpet_o[32:64, :] = kpt[64:96, :]
    vt = jnp.transpose(ckv).astype(BF16)
    tkb = vt_o.shape[-1]
    for u in range(vt_o.shape[0]):
        vt_o[u] = vt[:, u * tkb:(u + 1) * tkb]
    rv_o[...] = zv.astype(BF16)
    mq_o[...] = zm.astype(BF16)


def _project(x2, tabs, g, win, gq, wuq, wuk, gkv, batch, seq, tm, tkb):
    n, d = x2.shape
    nt = seq // tm
    row = lambda w: pl.BlockSpec((tm, w), lambda i: (i, 0))
    tab = pl.BlockSpec((tm, LANES), lambda i: (i % nt, 0))
    full = lambda a: pl.BlockSpec(a.shape, lambda i: (0,) * a.ndim)
    outs = [(RET_W, BF16), (RET_W, BF16), (RET_W, BF16), (N_HEADS * QK_W, BF16), (QK_W, BF16),
            (KV_LORA, F32), (MEM_W, BF16), (D_MIX, BF16)]
    kpet_spec = pl.BlockSpec((None, MLA_ROPE, tm), lambda i: (i // nt, 0, i % nt))
    vt_spec = pl.BlockSpec((None, tm // tkb, KV_LORA, tkb), lambda i: (i // nt, i % nt, 0, 0))
    return pl.pallas_call(
        _proj_kernel,
        grid=(n // tm,),
        in_specs=[row(d), tab, tab, tab, tab, full(g), full(win), full(gq), full(wuq), full(wuk), full(gkv)],
        out_specs=[row(w) for w, _ in outs] + [kpet_spec, vt_spec],
        out_shape=[jax.ShapeDtypeStruct((n, w), dt) for w, dt in outs]
                  + [jax.ShapeDtypeStruct((batch, MLA_ROPE, seq), F32),
                     jax.ShapeDtypeStruct((batch, seq // tkb, KV_LORA, tkb), BF16)],
        compiler_params=_cparams("parallel"),
        name="proj",
    )(x2, *tabs, g, win, gq, wuq, wuk, gkv)


def _ret_kernel(rq_ref, rk_ref, rv_ref, intra_ref, qdec_ref, kdec_ref, sdec_ref, gn_ref,
                ret_o, state_o, s_scr):
    c = pl.program_id(1)

    @pl.when(c == 0)
    def _():
        s_scr[...] = jnp.zeros_like(s_scr)

    chunk = intra_ref.shape[1]
    for ci in range(rq_ref.shape[0] // chunk):
        rs = slice(ci * chunk, (ci + 1) * chunk)
        heads = [slice(h * HEAD_DIM, (h + 1) * HEAD_DIM) for h in range(N_HEADS)]
        qk = [_dot_nt(rq_ref[rs, sl], rk_ref[rs, sl]) for sl in heads]
        qs = [_dot(rq_ref[rs, sl], s_scr[h].astype(BF16)) for h, sl in enumerate(heads)]
        for h, sl in enumerate(heads):
            k, v = rk_ref[rs, sl], rv_ref[rs, sl]
            o = _dot((qk[h] * intra_ref[h]).astype(BF16), v) + qdec_ref[h] * qs[h]
            kd_t = jnp.transpose(k.astype(F32) * kdec_ref[h]).astype(BF16)
            s_scr[h] = s_scr[h] * sdec_ref[h] + _dot(kd_t, v)
            mu = jnp.mean(o, axis=-1, keepdims=True)
            d = o - mu
            var = jnp.mean(d * d, axis=-1, keepdims=True)
            ret_o[rs, sl] = (d * lax.rsqrt(var + EPS) * gn_ref[:, sl]).astype(BF16)

    @pl.when(c == pl.num_programs(1) - 1)
    def _():
        state_o[...] = s_scr[...]


def _ret_tables(chunk):
    log_g = jnp.log1p(-jnp.exp2(-5.0 - jnp.arange(N_HEADS, dtype=F32)))
    idx = jnp.arange(chunk, dtype=F32)
    diff = idx[:, None] - idx[None, :]
    intra = jnp.where(diff[None] >= 0, jnp.exp(jnp.maximum(diff, 0.0)[None] * log_g[:, None, None]), 0.0)
    q_dec = jnp.exp((idx[None, :] + 1.0) * log_g[:, None])
    k_dec = jnp.exp((chunk - 1.0 - idx)[None, :] * log_g[:, None])
    s_dec = jnp.exp(chunk * log_g)
    bc = lambda a: jnp.broadcast_to(a[:, :, None], (N_HEADS, chunk, HEAD_DIM))
    return intra, bc(q_dec), bc(k_dec), jnp.broadcast_to(s_dec[:, None, None], (N_HEADS, 1, HEAD_DIM))


def _retention_prompt(rq, rk, rv, gn, batch, seq, tr):
    chunk = RET_CHUNK
    assert seq % tr == 0 and tr % chunk == 0
    nc = seq // tr
    intra, qdec, kdec, sdec = _ret_tables(chunk)
    row = pl.BlockSpec((tr, RET_W), lambda b, c: (b * nc + c, 0))
    full = lambda a: pl.BlockSpec(a.shape, lambda b, c: (0,) * a.ndim)
    return pl.pallas_call(
        _ret_kernel,
        grid=(batch, nc),
        in_specs=[row, row, row, full(intra), full(qdec), full(kdec), full(sdec), full(gn)],
        out_specs=[row, pl.BlockSpec((None, N_HEADS, HEAD_DIM, HEAD_DIM), lambda b, c: (b, 0, 0, 0))],
        out_shape=[jax.ShapeDtypeStruct((batch * seq, RET_W), BF16),
                   jax.ShapeDtypeStruct((batch, N_HEADS, HEAD_DIM, HEAD_DIM), F32)],
        scratch_shapes=[pltpu.VMEM((N_HEADS, HEAD_DIM, HEAD_DIM), F32)],
        compiler_params=_cparams("parallel", "arbitrary"),
        name="ret_prompt",
    )(rq, rk, rv, intra, qdec, kdec, sdec, gn)


def _mla_kernel(qa_ref, kv_ref, vt_ref, wuv_ref, mla_o, q_scr, m_scr, l_scr, acc_scr, *, tq):
    i = pl.program_id(1)
    cols = N_HEADS * tq
    c2 = MLA_SCALE * LOG2E
    for h in range(N_HEADS):
        q_scr[h * tq:(h + 1) * tq, :] = qa_ref[:, h * QK_W:(h + 1) * QK_W]
    def block(blk, off, nk, masked, q_lo=0, first=False):
        kj = kv_ref[pl.ds(pl.multiple_of(blk * tq + off, MLA_SUB), nk), :]
        nq = tq - q_lo
        if masked:
            kpos = blk * tq + off + lax.broadcasted_iota(jnp.int32, (nk, nq), 0)
            qpos = i * tq + q_lo + lax.broadcasted_iota(jnp.int32, (nk, nq), 1)
            keep = kpos <= qpos
        scores = [_dot_nt(kj, q_scr[h * tq + q_lo:(h + 1) * tq, :]) for h in range(N_HEADS)]
        for h in range(N_HEADS):
            cs = slice(h * tq + q_lo, (h + 1) * tq)
            s = scores[h]
            if masked:
                s = jnp.where(keep, s, NEG_BIG)
            m_new = jnp.max(s, axis=0, keepdims=True)
            if not first:
                m_old = m_scr[:, cs]
                m_new = jnp.maximum(m_old, m_new)
                alpha = jnp.exp2((m_old - m_new) * c2)
            p = jnp.exp2((s - m_new) * c2)
            l_new = jnp.sum(p, axis=0, keepdims=True)
            pb = p.astype(BF16)
            pv = None
            for u in range(0, nk, tq):
                w = min(tq, nk - u)
                part = _dot(vt_ref[blk + u // tq, :, off:off + w], pb[u:u + w, :])
                pv = part if pv is None else pv + part
            if first:
                l_scr[:, cs] = l_new
                acc_scr[:, cs] = pv
            else:
                l_scr[:, cs] = alpha * l_scr[:, cs] + l_new
                acc_scr[:, cs] = alpha * acc_scr[:, cs] + pv
            m_scr[:, cs] = m_new

    for u in range(tq // MLA_SUB):
        block(i, u * MLA_SUB, MLA_SUB, True, q_lo=u * MLA_SUB, first=(u == 0))

    def body(j, carry):
        block(2 * j, 0, 2 * tq, False)
        return carry

    lax.fori_loop(0, i // 2, body, 0)

    @pl.when(i % 2 == 1)
    def _():
        block(i - 1, 0, tq, False)
    lat_t = acc_scr[...] / l_scr[...]
    for h in range(N_HEADS):
        lat = jnp.transpose(lat_t[:, h * tq:(h + 1) * tq]).astype(BF16)
        mla_o[:, h * MLA_V:(h + 1) * MLA_V] = _dot(lat, wuv_ref[h]).astype(BF16)


def _mla_prompt(qa, kv, vt, wuv, batch, seq, tq):
    nq = seq // tq
    assert vt.shape == (batch, nq, KV_LORA, tq)
    return pl.pallas_call(
        functools.partial(_mla_kernel, tq=tq),
        grid=(batch, nq),
        in_specs=[pl.BlockSpec((tq, N_HEADS * QK_W), lambda b, i: (b * nq + i, 0)),
                  pl.BlockSpec((seq, QK_W), lambda b, i: (b, 0)),
                  pl.BlockSpec((None, nq, KV_LORA, tq), lambda b, i: (b, 0, 0, 0)),
                  pl.BlockSpec(wuv.shape, lambda b, i: (0, 0, 0))],
        out_specs=pl.BlockSpec((tq, N_HEADS * MLA_V), lambda b, i: (b * nq + i, 0)),
        out_shape=jax.ShapeDtypeStruct((batch * seq, N_HEADS * MLA_V), BF16),
        scratch_shapes=[pltpu.VMEM((N_HEADS * tq, QK_W), BF16), pltpu.VMEM((1, N_HEADS * tq), F32),
                        pltpu.VMEM((1, N_HEADS * tq), F32), pltpu.VMEM((KV_LORA, N_HEADS * tq), F32)],
        compiler_params=_cparams("parallel", "arbitrary"),
        name="mla_prompt",
    )(qa, kv, vt, wuv)


def _memkv_kernel(mem_ref, g_ref, w_ref, k_o, v_o, k4_o, v4_o):
    kvp = _dot(_rms(mem_ref[...], g_ref[...]).astype(BF16), w_ref[...])
    k_o[...] = kvp[:, 0:MEM_W]
    v_o[...] = kvp[:, MEM_W:2 * MEM_W]
    tm = mem_ref.shape[0]
    for h in range(N_HEADS):
        k4_o[pl.ds(h, tm, stride=N_HEADS), :] = kvp[:, h * HEAD_DIM:(h + 1) * HEAD_DIM]
        v4_o[pl.ds(h, tm, stride=N_HEADS), :] = kvp[:, MEM_W + h * HEAD_DIM:MEM_W + (h + 1) * HEAD_DIM]


def _mem_kv(mem2, g, w, tm):
    n, d = mem2.shape
    row = lambda wd: pl.BlockSpec((tm, wd), lambda i: (i, 0))
    row4 = pl.BlockSpec((tm * N_HEADS, HEAD_DIM), lambda i: (i, 0))
    full = lambda a: pl.BlockSpec(a.shape, lambda i: (0,) * a.ndim)
    return pl.pallas_call(
        _memkv_kernel,
        grid=(n // tm,),
        in_specs=[row(d), full(g), full(w)],
        out_specs=[row(MEM_W), row(MEM_W), row4, row4],
        out_shape=[jax.ShapeDtypeStruct((n, MEM_W), F32)] * 2
                  + [jax.ShapeDtypeStruct((n * N_HEADS, HEAD_DIM), F32)] * 2,
        compiler_params=_cparams("parallel"),
        name="mem_kv",
    )(mem2, g, w)


def _out_proj(x, cat_scr, wout_ref, gf_ref):
    return _rms(x + _dot(cat_scr[...], wout_ref[...]), gf_ref[...])


def _merge_prompt_kernel(x_ref, ret_ref, mla_ref, mq_ref, sg_ref, mk_ref, mv_ref, wout_ref, gf_ref,
                         y_o, cat_scr):
    cat_scr[:, 0:RET_W] = ret_ref[...] * sg_ref[:, 0:RET_W]
    cat_scr[:, RET_W:2 * RET_W] = mla_ref[...] * sg_ref[:, RET_W:2 * RET_W]
    heads = [slice(h * HEAD_DIM, (h + 1) * HEAD_DIM) for h in range(N_HEADS)]
    scores = [_dot_nt(mk_ref[:, sl].astype(BF16), mq_ref[:, sl]) for sl in heads]
    part = x_ref[...] + _dot(cat_scr[:, 0:2 * RET_W], wout_ref[0:2 * RET_W, :])
    for h, sl in enumerate(heads):
        s = scores[h] * MEM_SCALE
        p = jnp.exp(s - jnp.max(s, axis=0, keepdims=True))
        p = (p / jnp.sum(p, axis=0, keepdims=True)).astype(BF16)
        o_t = _dot(jnp.transpose(mv_ref[:, sl]).astype(BF16), p)
        gsl = slice(2 * RET_W + h * HEAD_DIM, 2 * RET_W + (h + 1) * HEAD_DIM)
        cat_scr[:, gsl] = (jnp.transpose(o_t) * sg_ref[:, gsl].astype(F32)).astype(BF16)
    half = x_ref.shape[0] // 2
    tails = [_dot(cat_scr[r * half:(r + 1) * half, 2 * RET_W:D_MIX], wout_ref[2 * RET_W:D_MIX, :]) for r in range(2)]
    for r in range(2):
        rows = slice(r * half, (r + 1) * half)
        y_o[rows, :] = _rms(part[rows, :] + tails[r], gf_ref[...])


def _merge_prompt(x2, ret_n, mla, mq, sg, mk, mv, wout, gf, batch, seq, tm):
    nt = seq // tm
    n_mem = mk.shape[0] // batch
    row = lambda w: pl.BlockSpec((tm, w), lambda b, t: (b * nt + t, 0))
    mem = pl.BlockSpec((n_mem, MEM_W), lambda b, t: (b, 0))
    full = lambda a: pl.BlockSpec(a.shape, lambda b, t: (0,) * a.ndim)
    return pl.pallas_call(
        _merge_prompt_kernel,
        grid=(batch, nt),
        in_specs=[row(x2.shape[1]), row(RET_W), row(RET_W), row(MEM_W), row(D_MIX), mem, mem, full(wout), full(gf)],
        out_specs=row(x2.shape[1]),
        out_shape=jax.ShapeDtypeStruct(x2.shape, F32),
        scratch_shapes=[pltpu.VMEM((tm, D_MIX), BF16)],
        compiler_params=_cparams("parallel", "arbitrary"),
        name="merge_prompt",
    )(x2, ret_n, mla, mq, sg, mk, mv, wout, gf)


def _merge_sample_kernel(x_ref, ret_ref, lat_ref, memo_ref, sg_ref, wuv_ref, wout_ref, gf_ref, y_o, cat_scr):
    cat_scr[:, 0:RET_W] = ret_ref[...] * sg_ref[:, 0:RET_W]
    for h in range(N_HEADS):
        lat = lat_ref[:, h * KV_LORA:(h + 1) * KV_LORA].astype(BF16)
        gsl = slice(RET_W + h * MLA_V, RET_W + (h + 1) * MLA_V)
        cat_scr[:, gsl] = (_dot(lat, wuv_ref[h]) * sg_ref[:, gsl].astype(F32)).astype(BF16)
    cat_scr[:, 2 * RET_W:D_MIX] = (memo_ref[...] * sg_ref[:, 2 * RET_W:D_MIX].astype(F32)).astype(BF16)
    y_o[...] = _out_proj(x_ref[...], cat_scr, wout_ref, gf_ref)


def _merge_sample(x2, ret_n, lat, memo, sg, wuv, wout, gf):
    args = (x2, ret_n, lat, memo, sg, wuv, wout, gf)
    return pl.pallas_call(
        _merge_sample_kernel,
        grid=(1,),
        in_specs=[pl.BlockSpec(a.shape, lambda i, nd=a.ndim: (0,) * nd) for a in args],
        out_specs=pl.BlockSpec(x2.shape, lambda i: (0, 0)),
        out_shape=jax.ShapeDtypeStruct(x2.shape, F32),
        scratch_shapes=[pltpu.VMEM((x2.shape[0], D_MIX), BF16)],
        compiler_params=_cparams("arbitrary"),
        name="merge_sample",
    )(*args)


def _ret_step_kernel(rq_ref, rk_ref, rv_ref, s_ref, gam_ref, gn_ref, ret_o, s_o, *, bt):
    sq = (HEAD_DIM, HEAD_DIM)
    heads = [slice(h * HEAD_DIM, (h + 1) * HEAD_DIM) for h in range(N_HEADS)]
    pairs = [(b, h) for b in range(bt) for h in range(N_HEADS)]
    qs = {(b, h): rq_ref[b:b + 1, heads[h]].astype(F32) for b, h in pairs}
    ks = {(b, h): rk_ref[b:b + 1, heads[h]].astype(F32) for b, h in pairs}
    q_cols = {bh: jnp.transpose(jnp.broadcast_to(qs[bh], sq)) for bh in pairs}
    k_cols = {bh: jnp.transpose(jnp.broadcast_to(ks[bh], sq)) for bh in pairs}
    for b, h in pairs:
        sl = heads[h]
        q, k = qs[b, h], ks[b, h]
        v = rv_ref[b:b + 1, sl].astype(F32)
        gam = gam_ref[h]
        s_old = s_ref[b, h]
        qk = jnp.sum(q * k, axis=-1, keepdims=True)
        o = qk * v + gam * jnp.sum(q_cols[b, h] * s_old, axis=0, keepdims=True)
        s_o[b, h] = s_old * gam + k_cols[b, h] * v
        mu = jnp.mean(o, axis=-1, keepdims=True)
        d = o - mu
        var = jnp.mean(d * d, axis=-1, keepdims=True)
        ret_o[b:b + 1, sl] = (d * lax.rsqrt(var + EPS) * gn_ref[:, sl]).astype(BF16)


def _retention_step(rq, rk, rv, state, gn, bt):
    n = rq.shape[0]
    log_g = jnp.log1p(-jnp.exp2(-5.0 - jnp.arange(N_HEADS, dtype=F32)))
    gam = jnp.broadcast_to(jnp.exp(log_g)[:, None, None], (N_HEADS, 1, HEAD_DIM))
    row = pl.BlockSpec((bt, RET_W), lambda i: (i, 0))
    st = pl.BlockSpec((bt, N_HEADS, HEAD_DIM, HEAD_DIM), lambda i: (i, 0, 0, 0))
    full = lambda a: pl.BlockSpec(a.shape, lambda i: (0,) * a.ndim)
    return pl.pallas_call(
        functools.partial(_ret_step_kernel, bt=bt),
        grid=(n // bt,),
        in_specs=[row, row, row, st, full(gam), full(gn)],
        out_specs=[row, st],
        out_shape=[jax.ShapeDtypeStruct((n, RET_W), BF16), jax.ShapeDtypeStruct(state.shape, F32)],
        compiler_params=_cparams("parallel"),
        name="ret_step",
    )(rq, rk, rv, state, gam, gn)


def _mla_dec_kernel(pt_ref, q_ref, kvn_ref, ckv_hbm, kpet_hbm, o_ref,
                    ckv_buf, kpe_buf, sems, m_scr, l_scr, acc_scr, *, cp, page, n_sub):
    b, j = pl.program_id(0), pl.program_id(1)
    nb, nj = pl.num_programs(0), pl.num_programs(1)
    total = nb * nj
    t = b * nj + j
    slot = t % N_SLOTS
    last = t == total - 1
    ahead = N_SLOTS - 1

    def coords(tt):
        ok = tt < total
        return jnp.where(ok, tt // nj, b), jnp.where(ok, tt % nj, j)

    b_next, j_next = coords(t + ahead)
    slot_next = (t + ahead) % N_SLOTS

    def page_copy(which, bb, jj, sl, r, lookup=True):
        pg = pt_ref[bb, jj * cp + r] if lookup else 0
        if which == 0:
            return pltpu.make_async_copy(ckv_hbm.at[pg], ckv_buf.at[sl, r], sems.at[0, sl])
        return pltpu.make_async_copy(kpet_hbm.at[pg], kpe_buf.at[sl, :, r * page:(r + 1) * page], sems.at[1, sl])

    def page_copies(bb, jj, sl, r, lookup=True):
        return tuple(page_copy(w, bb, jj, sl, r, lookup) for w in (0, 1))

    @pl.when(t == 0)
    def _():
        for tt in range(ahead):
            bb, jj = coords(tt)
            for r in range(cp):
                for cpy in page_copies(bb, jj, tt, r):
                    cpy.start()

    @pl.when(j == 0)
    def _():
        m_scr[...] = jnp.full_like(m_scr, NEG_BIG)
        l_scr[...] = jnp.zeros_like(l_scr)
        acc_scr[...] = jnp.zeros_like(acc_scr)

    q = q_ref[...]
    ql, qp = q[:, 0:KV_LORA], q[:, KV_LORA:KV_LORA + MLA_ROPE]
    c2 = MLA_SCALE * LOG2E

    def update(s_parts, v_loaders, between=None):
        m_old = m_scr[...]
        m_new = m_old
        for s in s_parts:
            m_new = jnp.maximum(m_new, jnp.max(s, axis=-1, keepdims=True))
        alpha = jnp.exp2((m_old - m_new) * c2)
        l = alpha * l_scr[...]
        acc = alpha * acc_scr[...]
        for u, (s, load_v) in enumerate(zip(s_parts, v_loaders)):
            p = jnp.exp2((s - m_new) * c2)
            l = l + jnp.sum(p, axis=-1, keepdims=True)
            acc = acc + _dot(p.astype(BF16), load_v())
            if between is not None:
                between(u)
        m_scr[...] = m_new
        l_scr[...] = l
        acc_scr[...] = acc

    for r in range(cp):
        for cpy in page_copies(b, j, slot, r, lookup=False):
            cpy.wait()
    per = cp // n_sub

    def load_keys(u):
        return ckv_buf[slot, u * per:(u + 1) * per].reshape(per * page, KV_LORA).astype(BF16)

    def start_next(u, which):
        for r in range(u * per, (u + 1) * per):
            page_copy(which, b_next, j_next, slot_next, r).start()

    s_parts = []
    for u in range(n_sub):
        ks = slice(u * per * page, (u + 1) * per * page)
        s_parts.append(_dot_nt(ql, load_keys(u)) + _dot(qp, kpe_buf[slot, :, ks].astype(BF16)))
        start_next(u, 0)
    update(s_parts, [functools.partial(load_keys, u) for u in range(n_sub)],
           between=lambda u: start_next(u, 1))

    @pl.when(j == nj - 1)
    def _():
        kvn = kvn_ref[...].astype(BF16)
        s = _dot_nt(ql, kvn[:, 0:KV_LORA]) + _dot_nt(qp, kvn[:, KV_LORA:KV_LORA + MLA_ROPE])
        s = jnp.where(lax.broadcasted_iota(jnp.int32, s.shape, 1) == 0, s, NEG_BIG)
        update([s], [lambda: kvn[:, 0:KV_LORA]])
        o_ref[...] = acc_scr[...] / l_scr[...]

    @pl.when(last)
    def _():
        for k in range(1, N_SLOTS):
            for r in range(cp):
                for cpy in page_copies(b, j, (t + k) % N_SLOTS, r, lookup=False):
                    cpy.wait()


def _mla_decode(q8, kvn8, pool_ckv, pool_kpet, page_table, cp):
    n, n_pages = page_table.shape
    page = pool_ckv.shape[1]
    qw = q8.shape[-1]
    spec_q = pl.BlockSpec((None, 8, qw), lambda b, j, pt: (b, 0, 0))
    grid_spec = pltpu.PrefetchScalarGridSpec(
        num_scalar_prefetch=1,
        grid=(n, n_pages // cp),
        in_specs=[spec_q, spec_q, pl.BlockSpec(memory_space=pl.ANY), pl.BlockSpec(memory_space=pl.ANY)],
        out_specs=pl.BlockSpec((None, 8, KV_LORA), lambda b, j, pt: (b, 0, 0)),
        scratch_shapes=[pltpu.VMEM((N_SLOTS, cp, page, KV_LORA), F32),
                        pltpu.VMEM((N_SLOTS, MLA_ROPE, cp * page), F32),
                        pltpu.SemaphoreType.DMA((2, N_SLOTS)),
                        pltpu.VMEM((8, 1), F32), pltpu.VMEM((8, 1), F32), pltpu.VMEM((8, KV_LORA), F32)],
    )
    return pl.pallas_call(
        functools.partial(_mla_dec_kernel, cp=cp, page=page, n_sub=1),
        grid_spec=grid_spec,
        out_shape=jax.ShapeDtypeStruct((n, 8, KV_LORA), F32),
        compiler_params=_cparams("arbitrary", "arbitrary"),
        name="mla_decode",
    )(page_table, q8, kvn8, pool_ckv, pool_kpet)


def _mem_dec_kernel(q_ref, mk_ref, mv_ref, o_ref, *, bt):
    n_col = mk_ref.shape[1]
    col_head = jnp.bitwise_and(lax.broadcasted_iota(jnp.int32, (8, n_col), 1), N_HEADS - 1)
    own = col_head == lax.broadcasted_iota(jnp.int32, (8, n_col), 0)
    pad = jnp.zeros((8 - N_HEADS, HEAD_DIM), BF16)
    scores = [_dot_nt(jnp.concatenate([q_ref[b], pad], axis=0), mk_ref[b].astype(BF16)) for b in range(bt)]
    probs = []
    for b in range(bt):
        s = jnp.where(own, scores[b] * MEM_SCALE, NEG_BIG)
        p = jnp.exp(s - jnp.max(s, axis=-1, keepdims=True))
        probs.append((p / jnp.sum(p, axis=-1, keepdims=True)).astype(BF16))
    for b in range(bt):
        o_ref[b] = _dot(probs[b], mv_ref[b].astype(BF16))[0:N_HEADS, :]


def _mem_decode(mq, mk, mv, bt):
    n, n_col, _ = mk.shape
    row = pl.BlockSpec((bt, N_HEADS, HEAD_DIM), lambda i: (i, 0, 0))
    mem = pl.BlockSpec((bt, n_col, HEAD_DIM), lambda i: (i, 0, 0))
    return pl.pallas_call(
        functools.partial(_mem_dec_kernel, bt=bt),
        grid=(n // bt,),
        in_specs=[row, mem, mem],
        out_specs=row,
        out_shape=jax.ShapeDtypeStruct((n, N_HEADS, HEAD_DIM), F32),
        compiler_params=_cparams("parallel"),
        name="mem_decode",
    )(mq, mk, mv)


def _tiles(seq, n_mem_rows, n_pages):
    return dict(
        tm=min(512, seq),
        tq=min(512, seq),
        t_merge=min(1024, seq),
        t_ret=min(2048, seq),
        t_mem=min(512, n_mem_rows),
        cp=min(64, n_pages),
        bt=8,
    )


def _rope_tables(pos, n_freq, slot):
    inv = ROPE_BASE ** (-jnp.arange(0, 2 * n_freq, 2, dtype=F32) / (2 * n_freq))
    ang = pos.astype(F32)[:, None] * inv[None, :]
    c, s = jnp.cos(ang), jnp.sin(ang)
    z = jnp.zeros((pos.shape[0], slot // 2 - n_freq), F32)
    return jnp.concatenate([c, z, c, z], axis=1), jnp.concatenate([-s, z, s, z], axis=1)


def _spread(w):
    z = jnp.zeros(w.shape[:-1] + (32,), w.dtype)
    return jnp.concatenate([w[..., 0:32], z, w[..., 32:64], z], axis=-1)


def _prep_weights(w_in, w_uq, w_uk, w_uv, w_mem_kv, w_out):
    o = [0, 512, 1024, 1536, 1920, 2176, 2240, 2752, 4288]
    seg = lambda i: w_in[:, o[i]:o[i + 1]]
    win = jnp.concatenate([seg(0), seg(1), seg(2), seg(3), _spread(seg(5)), seg(4), seg(6), seg(7)], axis=1)
    wuq = jnp.concatenate([w_uq[:, :, :MLA_NOPE].reshape(Q_LORA, -1),
                           _spread(w_uq[:, :, MLA_NOPE:]).reshape(Q_LORA, -1)], axis=1)
    wuk = jnp.transpose(w_uk, (1, 2, 0))
    wuv = jnp.transpose(w_uv, (1, 0, 2))
    return tuple(a.astype(BF16) for a in (win, wuq, wuk, wuv, w_mem_kv, w_out))


def kernel(x_prompt, x_sample, mem_prompt, cache_ckv, cache_kpe, page_table, state_ret, cache_mem_k, cache_mem_v,
           norm_g, w_in, ret_gn_g, mla_qnorm_g, w_uq, w_uk, mla_kvnorm_g, w_uv, mem_norm_g, w_mem_kv, w_out,
           final_norm_g):
    batch, seq, d_model = x_prompt.shape
    n_dec = x_sample.shape[0]
    n_mem = mem_prompt.shape[1]
    depth = w_in.shape[0]
    assert depth == 1 and x_sample.shape[1] == 1
    l = 0
    win, wuq, wuk, wuv, wmem, wout = _prep_weights(w_in[l], w_uq[l], w_uk[l], w_uv[l], w_mem_kv[l], w_out[l])
    g_in = norm_g[l][None, :]
    g_q = mla_qnorm_g[l][None, :]
    g_kv = mla_kvnorm_g[l][None, :]
    g_gn = ret_gn_g[l][None, :]
    g_mem = mem_norm_g[l][None, :]
    g_fin = final_norm_g[None, :]

    t = _tiles(seq, batch * n_mem, page_table.shape[1])
    tm, tq = t["tm"], t["tq"]

    xp = x_prompt.reshape(batch * seq, d_model)
    pos_p = jnp.arange(seq, dtype=jnp.int32)
    tabs_p = _rope_tables(pos_p, HEAD_DIM // 2, LANES) + _rope_tables(pos_p, MLA_ROPE // 2, LANES)
    rq, rk, rv, qa, kvb, ckv, mq, sg, kpet, vt = _project(
        xp, tabs_p, g_in, win, g_q, wuq, wuk, g_kv, batch, seq, tm, tq)
    ret_n, ret_state_p = _retention_prompt(rq, rk, rv, g_gn, batch, seq, t["t_ret"])
    mla = _mla_prompt(qa, kvb, vt, wuv, batch, seq, tq)
    mk, mv, mk4, mv4 = _mem_kv(mem_prompt.reshape(batch * n_mem, d_model), g_mem, wmem, t["t_mem"])
    y_p = _merge_prompt(xp, ret_n, mla, mq, sg, mk, mv, wout, g_fin, batch, seq, t["t_merge"])

    xs = x_sample.reshape(n_dec, d_model)
    pos_s = jnp.full((n_dec,), PAST_LEN, dtype=jnp.int32)
    tabs_s = _rope_tables(pos_s, HEAD_DIM // 2, LANES) + _rope_tables(pos_s, MLA_ROPE // 2, LANES)
    rq_s, rk_s, rv_s, qa_s, _, ckv_s, mq_s, sg_s, kpet_s, _ = _project(
        xs, tabs_s, g_in, win, g_q, wuq, wuk, g_kv, 1, n_dec, n_dec, n_dec)
    kpe_s = jnp.swapaxes(kpet_s[0], 0, 1)
    ret_n_s, ret_state_s = _retention_step(rq_s, rk_s, rv_s, state_ret[l], g_gn, t["bt"])
    qh = qa_s.reshape(n_dec, N_HEADS, QK_W)
    q_std = jnp.concatenate([qh[..., 0:KV_LORA], qh[..., KV_LORA:KV_LORA + 32], qh[..., KV_LORA + 64:KV_LORA + 96]], -1)
    q8 = jnp.pad(q_std, ((0, 0), (0, 8 - N_HEADS), (0, 0)))
    kvn8 = jnp.pad(jnp.concatenate([ckv_s, kpe_s], axis=-1)[:, None, :], ((0, 0), (0, 7), (0, 0)))
    lat_s = _mla_decode(q8, kvn8, cache_ckv[l], jnp.swapaxes(cache_kpe[l], 1, 2), page_table, t["cp"])
    lat_s = lat_s[:, 0:N_HEADS, :].reshape(n_dec, N_HEADS * KV_LORA)
    memo_s = _mem_decode(mq_s.reshape(n_dec, N_HEADS, HEAD_DIM),
                         cache_mem_k[l].reshape(n_dec, n_mem * N_HEADS, HEAD_DIM),
                         cache_mem_v[l].reshape(n_dec, n_mem * N_HEADS, HEAD_DIM), t["bt"]).reshape(n_dec, MEM_W)
    y_s = _merge_sample(xs, ret_n_s, lat_s, memo_s, sg_s, wuv, wout, g_fin)

    return (y_p.reshape(batch, seq, d_model), y_s.reshape(n_dec, 1, d_model),
            ckv.reshape(1, batch, seq, KV_LORA), jnp.swapaxes(kpet, 1, 2)[None],
            ret_state_p[None], mk4.reshape(1, batch, n_mem, N_HEADS, HEAD_DIM), mv4.reshape(1, batch, n_mem, N_HEADS, HEAD_DIM),
            ckv_s.reshape(1, n_dec, 1, KV_LORA), kpe_s.reshape(1, n_dec, 1, MLA_ROPE), ret_state_s[None])
```

```python
import functools

import jax
import jax.numpy as jnp
from jax import lax
from jax.experimental import pallas as pl
from jax.experimental.pallas import tpu as pltpu

F32 = jnp.float32
BF16 = jnp.bfloat16

HEAD_DIM = 128
N_HEADS = 4
RET_W = N_HEADS * HEAD_DIM
MLA_NOPE = 128
MLA_ROPE = 64
MLA_V = 128
Q_LORA = 384
KV_LORA = 256
MEM_W = N_HEADS * HEAD_DIM
D_MIX = 3 * RET_W
RET_CHUNK = 256
PAST_LEN = 16384
ROPE_BASE = 10000.0
EPS = 1e-6
MLA_SCALE = (MLA_NOPE + MLA_ROPE) ** -0.5
MEM_SCALE = HEAD_DIM ** -0.5
RK_SCALE = HEAD_DIM ** -0.5
NEG_BIG = -1e30
LOG2E = 1.4426950408889634
N_SLOTS = 3
MLA_SUB = 256

LANES = 128
QK_W = KV_LORA + LANES
OFF_RQ, OFF_RK, OFF_RV = 0, RET_W, 2 * RET_W
OFF_CQ = 3 * RET_W
OFF_KPE = OFF_CQ + Q_LORA
OFF_CKV = OFF_KPE + LANES
OFF_MQ = OFF_CKV + KV_LORA
OFF_GATE = OFF_MQ + MEM_W
D_IN2 = OFF_GATE + D_MIX

VMEM_LIMIT = 48 * 1024 * 1024


def _cparams(*sem):
    return pltpu.CompilerParams(dimension_semantics=sem, vmem_limit_bytes=VMEM_LIMIT)


def _rms(x, g):
    return x * lax.rsqrt(jnp.mean(x * x, axis=-1, keepdims=True) + EPS) * g


def _dot(a, b):
    return jnp.dot(a, b, preferred_element_type=F32)


def _dot_nt(a, b):
    return lax.dot_general(a, b, (((1,), (1,)), ((), ())), preferred_element_type=F32)


def _rot_half(x, cos, sin):
    return x * cos + pltpu.roll(x, 64, 1) * sin


def _proj_kernel(x_ref, cr_ref, sr_ref, cm_ref, sm_ref, g_ref, win_ref, gq_ref, wuq_ref, wuk_ref, gkv_ref,
                 rq_o, rk_o, rv_o, qa_o, kv_o, ckv_o, mq_o, sg_o, kpet_o, vt_o):
    xn = _rms(x_ref[...], g_ref[...]).astype(BF16)
    cr, sr = cr_ref[...], sr_ref[...]
    cm, sm = cm_ref[...], sm_ref[...]

    seg = lambda off, w: _dot(xn, win_ref[:, off:off + w])
    zck = seg(OFF_CQ, Q_LORA + LANES)
    zcq, zkpe = zck[:, 0:Q_LORA], zck[:, Q_LORA:Q_LORA + LANES]
    gate = seg(OFF_GATE, D_MIX)
    cq = _rms(zcq, gq_ref[...]).astype(BF16)
    zq, zk = seg(OFF_RQ, RET_W), seg(OFF_RK, RET_W)
    q = _dot(cq, wuq_ref[...])
    sg_o[...] = (gate / (1.0 + jnp.exp(-gate))).astype(BF16)
    zckv = seg(OFF_CKV, KV_LORA)
    for h in range(N_HEADS):
        qn = q[:, h * MLA_NOPE:(h + 1) * MLA_NOPE].astype(BF16)
        qa_o[:, h * QK_W:h * QK_W + KV_LORA] = _dot(qn, wuk_ref[h]).astype(BF16)
        qp = q[:, RET_W + h * LANES:RET_W + (h + 1) * LANES]
        qa_o[:, h * QK_W + KV_LORA:(h + 1) * QK_W] = _rot_half(qp, cm, sm).astype(BF16)
    zv, zm = seg(OFF_RV, RET_W), seg(OFF_MQ, MEM_W)
    for h in range(N_HEADS):
        sl = slice(h * HEAD_DIM, (h + 1) * HEAD_DIM)
        rq_o[:, sl] = _rot_half(zq[:, sl], cr, sr).astype(BF16)
        rk_o[:, sl] = (_rot_half(zk[:, sl], cr, sr) * RK_SCALE).astype(BF16)

    ckv = _rms(zckv, gkv_ref[...])
    ckv_o[...] = ckv
    kv_o[:, 0:KV_LORA] = ckv.astype(BF16)
    kp = _rot_half(zkpe, cm, sm)
    kv_o[:, KV_LORA:QK_W] = kp.astype(BF16)
    kpt = jnp.transpose(kp)
    kpet_o[0:32, :] = kpt[0:32, :]
    kpet_o[32:64, :] = kpt[64:96, :]
    vt = jnp.transpose(ckv).astype(BF16)
    tkb = vt_o.shape[-1]
    for u in range(vt_o.shape[0]):
        vt_o[u] = vt[:, u * tkb:(u + 1) * tkb]
    rv_o[...] = zv.astype(BF16)
    mq_o[...] = zm.astype(BF16)


def _project(x2, tabs, g, win, gq, wuq, wuk, gkv, batch, seq, tm, tkb):
    n, d = x2.shape
    nt = seq // tm
    row = lambda w: pl.BlockSpec((tm, w), lambda i: (i, 0))
    tab = pl.BlockSpec((tm, LANES), lambda i: (i % nt, 0))
    full = lambda a: pl.BlockSpec(a.shape, lambda i: (0,) * a.ndim)
    outs = [(RET_W, BF16), (RET_W, BF16), (RET_W, BF16), (N_HEADS * QK_W, BF16), (QK_W, BF16),
            (KV_LORA, F32), (MEM_W, BF16), (D_MIX, BF16)]
    kpet_spec = pl.BlockSpec((None, MLA_ROPE, tm), lambda i: (i // nt, 0, i % nt))
    vt_spec = pl.BlockSpec((None, tm // tkb, KV_LORA, tkb), lambda i: (i // nt, i % nt, 0, 0))
    return pl.pallas_call(
        _proj_kernel,
        grid=(n // tm,),
        in_specs=[row(d), tab, tab, tab, tab, full(g), full(win), full(gq), full(wuq), full(wuk), full(gkv)],
        out_specs=[row(w) for w, _ in outs] + [kpet_spec, vt_spec],
        out_shape=[jax.ShapeDtypeStruct((n, w), dt) for w, dt in outs]
                  + [jax.ShapeDtypeStruct((batch, MLA_ROPE, seq), F32),
                     jax.ShapeDtypeStruct((batch, seq // tkb, KV_LORA, tkb), BF16)],
        compiler_params=_cparams("parallel"),
        name="proj",
    )(x2, *tabs, g, win, gq, wuq, wuk, gkv)


def _ret_kernel(rq_ref, rk_ref, rv_ref, intra_ref, qdec_ref, kdec_ref, sdec_ref, gn_ref,
                ret_o, state_o, s_scr):
    c = pl.program_id(1)

    @pl.when(c == 0)
    def _():
        s_scr[...] = jnp.zeros_like(s_scr)

    chunk = intra_ref.shape[1]
    for ci in range(rq_ref.shape[0] // chunk):
        rs = slice(ci * chunk, (ci + 1) * chunk)
        heads = [slice(h * HEAD_DIM, (h + 1) * HEAD_DIM) for h in range(N_HEADS)]
        qk = [_dot_nt(rq_ref[rs, sl], rk_ref[rs, sl]) for sl in heads]
        qs = [_dot(rq_ref[rs, sl], s_scr[h].astype(BF16)) for h, sl in enumerate(heads)]
        for h, sl in enumerate(heads):
            k, v = rk_ref[rs, sl], rv_ref[rs, sl]
            o = _dot((qk[h] * intra_ref[h]).astype(BF16), v) + qdec_ref[h] * qs[h]
            kd_t = jnp.transpose(k.astype(F32) * kdec_ref[h]).astype(BF16)
            s_scr[h] = s_scr[h] * sdec_ref[h] + _dot(kd_t, v)
            mu = jnp.mean(o, axis=-1, keepdims=True)
            d = o - mu
            var = jnp.mean(d * d, axis=-1, keepdims=True)
            ret_o[rs, sl] = (d * lax.rsqrt(var + EPS) * gn_ref[:, sl]).astype(BF16)

    @pl.when(c == pl.num_programs(1) - 1)
    def _():
        state_o[...] = s_scr[...]


def _ret_tables(chunk):
    log_g = jnp.log1p(-jnp.exp2(-5.0 - jnp.arange(N_HEADS, dtype=F32)))
    idx = jnp.arange(chunk, dtype=F32)
    diff = idx[:, None] - idx[None, :]
    intra = jnp.where(diff[None] >= 0, jnp.exp(jnp.maximum(diff, 0.0)[None] * log_g[:, None, None]), 0.0)
    q_dec = jnp.exp((idx[None, :] + 1.0) * log_g[:, None])
    k_dec = jnp.exp((chunk - 1.0 - idx)[None, :] * log_g[:, None])
    s_dec = jnp.exp(chunk * log_g)
    bc = lambda a: jnp.broadcast_to(a[:, :, None], (N_HEADS, chunk, HEAD_DIM))
    return intra, bc(q_dec), bc(k_dec), jnp.broadcast_to(s_dec[:, None, None], (N_HEADS, 1, HEAD_DIM))


def _retention_prompt(rq, rk, rv, gn, batch, seq, tr):
    chunk = RET_CHUNK
    assert seq % tr == 0 and tr % chunk == 0
    nc = seq // tr
    intra, qdec, kdec, sdec = _ret_tables(chunk)
    row = pl.BlockSpec((tr, RET_W), lambda b, c: (b * nc + c, 0))
    full = lambda a: pl.BlockSpec(a.shape, lambda b, c: (0,) * a.ndim)
    return pl.pallas_call(
        _ret_kernel,
        grid=(batch, nc),
        in_specs=[row, row, row, full(intra), full(qdec), full(kdec), full(sdec), full(gn)],
        out_specs=[row, pl.BlockSpec((None, N_HEADS, HEAD_DIM, HEAD_DIM), lambda b, c: (b, 0, 0, 0))],
        out_shape=[jax.ShapeDtypeStruct((batch * seq, RET_W), BF16),
                   jax.ShapeDtypeStruct((batch, N_HEADS, HEAD_DIM, HEAD_DIM), F32)],
        scratch_shapes=[pltpu.VMEM((N_HEADS, HEAD_DIM, HEAD_DIM), F32)],
        compiler_params=_cparams("parallel", "arbitrary"),
        name="ret_prompt",
    )(rq, rk, rv, intra, qdec, kdec, sdec, gn)


def _mla_kernel(qa_ref, kv_ref, vt_ref, wuv_ref, mla_o, q_scr, m_scr, l_scr, acc_scr, *, tq):
    i = pl.program_id(1)
    cols = N_HEADS * tq
    c2 = MLA_SCALE * LOG2E
    for h in range(N_HEADS):
        q_scr[h * tq:(h + 1) * tq, :] = qa_ref[:, h * QK_W:(h + 1) * QK_W]
    def block(blk, off, nk, masked, q_lo=0, first=False):
        kj = kv_ref[pl.ds(pl.multiple_of(blk * tq + off, MLA_SUB), nk), :]
        nq = tq - q_lo
        if masked:
            kpos = blk * tq + off + lax.broadcasted_iota(jnp.int32, (nk, nq), 0)
            qpos = i * tq + q_lo + lax.broadcasted_iota(jnp.int32, (nk, nq), 1)
            keep = kpos <= qpos
        scores = [_dot_nt(kj, q_scr[h * tq + q_lo:(h + 1) * tq, :]) for h in range(N_HEADS)]
        for h in range(N_HEADS):
            cs = slice(h * tq + q_lo, (h + 1) * tq)
            s = scores[h]
            if masked:
                s = jnp.where(keep, s, NEG_BIG)
            m_new = jnp.max(s, axis=0, keepdims=True)
            if not first:
                m_old = m_scr[:, cs]
                m_new = jnp.maximum(m_old, m_new)
                alpha = jnp.exp2((m_old - m_new) * c2)
            p = jnp.exp2((s - m_new) * c2)
            l_new = jnp.sum(p, axis=0, keepdims=True)
            pb = p.astype(BF16)
            pv = None
            for u in range(0, nk, tq):
                w = min(tq, nk - u)
                part = _dot(vt_ref[blk + u // tq, :, off:off + w], pb[u:u + w, :])
                pv = part if pv is None else pv + part
            if first:
                l_scr[:, cs] = l_new
                acc_scr[:, cs] = pv
            else:
                l_scr[:, cs] = alpha * l_scr[:, cs] + l_new
                acc_scr[:, cs] = alpha * acc_scr[:, cs] + pv
            m_scr[:, cs] = m_new

    for u in range(tq // MLA_SUB):
        block(i, u * MLA_SUB, MLA_SUB, True, q_lo=u * MLA_SUB, first=(u == 0))

    def body(j, carry):
        block(2 * j, 0, 2 * tq, False)
        return carry

    lax.fori_loop(0, i // 2, body, 0)

    @pl.when(i % 2 == 1)
    def _():
        block(i - 1, 0, tq, False)
    lat_t = acc_scr[...] / l_scr[...]
    for h in range(N_HEADS):
        lat = jnp.transpose(lat_t[:, h * tq:(h + 1) * tq]).astype(BF16)
        mla_o[:, h * MLA_V:(h + 1) * MLA_V] = _dot(lat, wuv_ref[h]).astype(BF16)


def _mla_prompt(qa, kv, vt, wuv, batch, seq, tq):
    nq = seq // tq
    assert vt.shape == (batch, nq, KV_LORA, tq)
    return pl.pallas_call(
        functools.partial(_mla_kernel, tq=tq),
        grid=(batch, nq),
        in_specs=[pl.BlockSpec((tq, N_HEADS * QK_W), lambda b, i: (b * nq + i, 0)),
                  pl.BlockSpec((seq, QK_W), lambda b, i: (b, 0)),
                  pl.BlockSpec((None, nq, KV_LORA, tq), lambda b, i: (b, 0, 0, 0)),
                  pl.BlockSpec(wuv.shape, lambda b, i: (0, 0, 0))],
        out_specs=pl.BlockSpec((tq, N_HEADS * MLA_V), lambda b, i: (b * nq + i, 0)),
        out_shape=jax.ShapeDtypeStruct((batch * seq, N_HEADS * MLA_V), BF16),
        scratch_shapes=[pltpu.VMEM((N_HEADS * tq, QK_W), BF16), pltpu.VMEM((1, N_HEADS * tq), F32),
                        pltpu.VMEM((1, N_HEADS * tq), F32), pltpu.VMEM((KV_LORA, N_HEADS * tq), F32)],
        compiler_params=_cparams("parallel", "arbitrary"),
        name="mla_prompt",
    )(qa, kv, vt, wuv)


def _memkv_kernel(mem_ref, g_ref, w_ref, k_o, v_o, k4_o, v4_o):
    kvp = _dot(_rms(mem_ref[...], g_ref[...]).astype(BF16), w_ref[...])
    k_o[...] = kvp[:, 0:MEM_W]
    v_o[...] = kvp[:, MEM_W:2 * MEM_W]
    tm = mem_ref.shape[0]
    for h in range(N_HEADS):
        k4_o[pl.ds(h, tm, stride=N_HEADS), :] = kvp[:, h * HEAD_DIM:(h + 1) * HEAD_DIM]
        v4_o[pl.ds(h, tm, stride=N_HEADS), :] = kvp[:, MEM_W + h * HEAD_DIM:MEM_W + (h + 1) * HEAD_DIM]


def _mem_kv(mem2, g, w, tm):
    n, d = mem2.shape
    row = lambda wd: pl.BlockSpec((tm, wd), lambda i: (i, 0))
    row4 = pl.BlockSpec((tm * N_HEADS, HEAD_DIM), lambda i: (i, 0))
    full = lambda a: pl.BlockSpec(a.shape, lambda i: (0,) * a.ndim)
    return pl.pallas_call(
        _memkv_kernel,
        grid=(n // tm,),
        in_specs=[row(d), full(g), full(w)],
        out_specs=[row(MEM_W), row(MEM_W), row4, row4],
        out_shape=[jax.ShapeDtypeStruct((n, MEM_W), F32)] * 2
                  + [jax.ShapeDtypeStruct((n * N_HEADS, HEAD_DIM), F32)] * 2,
        compiler_params=_cparams("parallel"),
        name="mem_kv",
    )(mem2, g, w)


def _out_proj(x, cat_scr, wout_ref, gf_ref):
    return _rms(x + _dot(cat_scr[...], wout_ref[...]), gf_ref[...])


def _merge_prompt_kernel(x_ref, ret_ref, mla_ref, mq_ref, sg_ref, mk_ref, mv_ref, wout_ref, gf_ref,
                         y_o, cat_scr):
    cat_scr[:, 0:RET_W] = ret_ref[...] * sg_ref[:, 0:RET_W]
    cat_scr[:, RET_W:2 * RET_W] = mla_ref[...] * sg_ref[:, RET_W:2 * RET_W]
    heads = [slice(h * HEAD_DIM, (h + 1) * HEAD_DIM) for h in range(N_HEADS)]
    scores = [_dot_nt(mk_ref[:, sl].astype(BF16), mq_ref[:, sl]) for sl in heads]
    part = x_ref[...] + _dot(cat_scr[:, 0:2 * RET_W], wout_ref[0:2 * RET_W, :])
    for h, sl in enumerate(heads):
        s = scores[h] * MEM_SCALE
        p = jnp.exp(s - jnp.max(s, axis=0, keepdims=True))
        p = (p / jnp.sum(p, axis=0, keepdims=True)).astype(BF16)
        o_t = _dot(jnp.transpose(mv_ref[:, sl]).astype(BF16), p)
        gsl = slice(2 * RET_W + h * HEAD_DIM, 2 * RET_W + (h + 1) * HEAD_DIM)
        cat_scr[:, gsl] = (jnp.transpose(o_t) * sg_ref[:, gsl].astype(F32)).astype(BF16)
    half = x_ref.shape[0] // 2
    tails = [_dot(cat_scr[r * half:(r + 1) * half, 2 * RET_W:D_MIX], wout_ref[2 * RET_W:D_MIX, :]) for r in range(2)]
    for r in range(2):
        rows = slice(r * half, (r + 1) * half)
        y_o[rows, :] = _rms(part[rows, :] + tails[r], gf_ref[...])


def _merge_prompt(x2, ret_n, mla, mq, sg, mk, mv, wout, gf, batch, seq, tm):
    nt = seq // tm
    n_mem = mk.shape[0] // batch
    row = lambda w: pl.BlockSpec((tm, w), lambda b, t: (b * nt + t, 0))
    mem = pl.BlockSpec((n_mem, MEM_W), lambda b, t: (b, 0))
    full = lambda a: pl.BlockSpec(a.shape, lambda b, t: (0,) * a.ndim)
    return pl.pallas_call(
        _merge_prompt_kernel,
        grid=(batch, nt),
        in_specs=[row(x2.shape[1]), row(RET_W), row(RET_W), row(MEM_W), row(D_MIX), mem, mem, full(wout), full(gf)],
        out_specs=row(x2.shape[1]),
        out_shape=jax.ShapeDtypeStruct(x2.shape, F32),
        scratch_shapes=[pltpu.VMEM((tm, D_MIX), BF16)],
        compiler_params=_cparams("parallel", "arbitrary"),
        name="merge_prompt",
    )(x2, ret_n, mla, mq, sg, mk, mv, wout, gf)


def _merge_sample_kernel(x_ref, ret_ref, lat_ref, memo_ref, sg_ref, wuv_ref, wout_ref, gf_ref, y_o, cat_scr):
    cat_scr[:, 0:RET_W] = ret_ref[...] * sg_ref[:, 0:RET_W]
    for h in range(N_HEADS):
        lat = lat_ref[:, h * KV_LORA:(h + 1) * KV_LORA].astype(BF16)
        gsl = slice(RET_W + h * MLA_V, RET_W + (h + 1) * MLA_V)
        cat_scr[:, gsl] = (_dot(lat, wuv_ref[h]) * sg_ref[:, gsl].astype(F32)).astype(BF16)
    cat_scr[:, 2 * RET_W:D_MIX] = (memo_ref[...] * sg_ref[:, 2 * RET_W:D_MIX].astype(F32)).astype(BF16)
    y_o[...] = _out_proj(x_ref[...], cat_scr, wout_ref, gf_ref)


def _merge_sample(x2, ret_n, lat, memo, sg, wuv, wout, gf):
    args = (x2, ret_n, lat, memo, sg, wuv, wout, gf)
    return pl.pallas_call(
        _merge_sample_kernel,
        grid=(1,),
        in_specs=[pl.BlockSpec(a.shape, lambda i, nd=a.ndim: (0,) * nd) for a in args],
        out_specs=pl.BlockSpec(x2.shape, lambda i: (0, 0)),
        out_shape=jax.ShapeDtypeStruct(x2.shape, F32),
        scratch_shapes=[pltpu.VMEM((x2.shape[0], D_MIX), BF16)],
        compiler_params=_cparams("arbitrary"),
        name="merge_sample",
    )(*args)


def _ret_step_kernel(rq_ref, rk_ref, rv_ref, s_ref, gam_ref, gn_ref, ret_o, s_o, *, bt):
    sq = (HEAD_DIM, HEAD_DIM)
    heads = [slice(h * HEAD_DIM, (h + 1) * HEAD_DIM) for h in range(N_HEADS)]
    pairs = [(b, h) for b in range(bt) for h in range(N_HEADS)]
    qs = {(b, h): rq_ref[b:b + 1, heads[h]].astype(F32) for b, h in pairs}
    ks = {(b, h): rk_ref[b:b + 1, heads[h]].astype(F32) for b, h in pairs}
    q_cols = {bh: jnp.transpose(jnp.broadcast_to(qs[bh], sq)) for bh in pairs}
    k_cols = {bh: jnp.transpose(jnp.broadcast_to(ks[bh], sq)) for bh in pairs}
    for b, h in pairs:
        sl = heads[h]
        q, k = qs[b, h], ks[b, h]
        v = rv_ref[b:b + 1, sl].astype(F32)
        gam = gam_ref[h]
        s_old = s_ref[b, h]
        qk = jnp.sum(q * k, axis=-1, keepdims=True)
        o = qk * v + gam * jnp.sum(q_cols[b, h] * s_old, axis=0, keepdims=True)
        s_o[b, h] = s_old * gam + k_cols[b, h] * v
        mu = jnp.mean(o, axis=-1, keepdims=True)
        d = o - mu
        var = jnp.mean(d * d, axis=-1, keepdims=True)
        ret_o[b:b + 1, sl] = (d * lax.rsqrt(var + EPS) * gn_ref[:, sl]).astype(BF16)


def _retention_step(rq, rk, rv, state, gn, bt):
    n = rq.shape[0]
    log_g = jnp.log1p(-jnp.exp2(-5.0 - jnp.arange(N_HEADS, dtype=F32)))
    gam = jnp.broadcast_to(jnp.exp(log_g)[:, None, None], (N_HEADS, 1, HEAD_DIM))
    row = pl.BlockSpec((bt, RET_W), lambda i: (i, 0))
    st = pl.BlockSpec((bt, N_HEADS, HEAD_DIM, HEAD_DIM), lambda i: (i, 0, 0, 0))
    full = lambda a: pl.BlockSpec(a.shape, lambda i: (0,) * a.ndim)
    return pl.pallas_call(
        functools.partial(_ret_step_kernel, bt=bt),
        grid=(n // bt,),
        in_specs=[row, row, row, st, full(gam), full(gn)],
        out_specs=[row, st],
        out_shape=[jax.ShapeDtypeStruct((n, RET_W), BF16), jax.ShapeDtypeStruct(state.shape, F32)],
        compiler_params=_cparams("parallel"),
        name="ret_step",
    )(rq, rk, rv, state, gam, gn)


def _mla_dec_kernel(pt_ref, q_ref, kvn_ref, ckv_hbm, kpet_hbm, o_ref,
                    ckv_buf, kpe_buf, sems, m_scr, l_scr, acc_scr, *, cp, page, n_sub):
    b, j = pl.program_id(0), pl.program_id(1)
    nb, nj = pl.num_programs(0), pl.num_programs(1)
    total = nb * nj
    t = b * nj + j
    slot = t % N_SLOTS
    last = t == total - 1
    ahead = N_SLOTS - 1

    def coords(tt):
        ok = tt < total
        return jnp.where(ok, tt // nj, b), jnp.where(ok, tt % nj, j)

    b_next, j_next = coords(t + ahead)
    slot_next = (t + ahead) % N_SLOTS

    def page_copy(which, bb, jj, sl, r, lookup=True):
        pg = pt_ref[bb, jj * cp + r] if lookup else 0
        if which == 0:
            return pltpu.make_async_copy(ckv_hbm.at[pg], ckv_buf.at[sl, r], sems.at[0, sl])
        return pltpu.make_async_copy(kpet_hbm.at[pg], kpe_buf.at[sl, :, r * page:(r + 1) * page], sems.at[1, sl])

    def page_copies(bb, jj, sl, r, lookup=True):
        return tuple(page_copy(w, bb, jj, sl, r, lookup) for w in (0, 1))

    @pl.when(t == 0)
    def _():
        for tt in range(ahead):
            bb, jj = coords(tt)
            for r in range(cp):
                for cpy in page_copies(bb, jj, tt, r):
                    cpy.start()

    @pl.when(j == 0)
    def _():
        m_scr[...] = jnp.full_like(m_scr, NEG_BIG)
        l_scr[...] = jnp.zeros_like(l_scr)
        acc_scr[...] = jnp.zeros_like(acc_scr)

    q = q_ref[...]
    ql, qp = q[:, 0:KV_LORA], q[:, KV_LORA:KV_LORA + MLA_ROPE]
    c2 = MLA_SCALE * LOG2E

    def update(s_parts, v_loaders, between=None):
        m_old = m_scr[...]
        m_new = m_old
        for s in s_parts:
            m_new = jnp.maximum(m_new, jnp.max(s, axis=-1, keepdims=True))
        alpha = jnp.exp2((m_old - m_new) * c2)
        l = alpha * l_scr[...]
        acc = alpha * acc_scr[...]
        for u, (s, load_v) in enumerate(zip(s_parts, v_loaders)):
            p = jnp.exp2((s - m_new) * c2)
            l = l + jnp.sum(p, axis=-1, keepdims=True)
            acc = acc + _dot(p.astype(BF16), load_v())
            if between is not None:
                between(u)
        m_scr[...] = m_new
        l_scr[...] = l
        acc_scr[...] = acc

    for r in range(cp):
        for cpy in page_copies(b, j, slot, r, lookup=False):
            cpy.wait()
    per = cp // n_sub

    def load_keys(u):
        return ckv_buf[slot, u * per:(u + 1) * per].reshape(per * page, KV_LORA).astype(BF16)

    def start_next(u, which):
        for r in range(u * per, (u + 1) * per):
            page_copy(which, b_next, j_next, slot_next, r).start(priority=which)

    s_parts = []
    for u in range(n_sub):
        ks = slice(u * per * page, (u + 1) * per * page)
        s_parts.append(_dot_nt(ql, load_keys(u)) + _dot(qp, kpe_buf[slot, :, ks].astype(BF16)))
        start_next(u, 0)
    update(s_parts, [functools.partial(load_keys, u) for u in range(n_sub)],
           between=lambda u: start_next(u, 1))

    @pl.when(j == nj - 1)
    def _():
        kvn = kvn_ref[...].astype(BF16)
        s = _dot_nt(ql, kvn[:, 0:KV_LORA]) + _dot_nt(qp, kvn[:, KV_LORA:KV_LORA + MLA_ROPE])
        s = jnp.where(lax.broadcasted_iota(jnp.int32, s.shape, 1) == 0, s, NEG_BIG)
        update([s], [lambda: kvn[:, 0:KV_LORA]])
        o_ref[...] = acc_scr[...] / l_scr[...]

    @pl.when(last)
    def _():
        for k in range(1, N_SLOTS):
            for r in range(cp):
                for cpy in page_copies(b, j, (t + k) % N_SLOTS, r, lookup=False):
                    cpy.wait()


def _mla_decode(q8, kvn8, pool_ckv, pool_kpet, page_table, cp):
    n, n_pages = page_table.shape
    page = pool_ckv.shape[1]
    qw = q8.shape[-1]
    spec_q = pl.BlockSpec((None, 8, qw), lambda b, j, pt: (b, 0, 0))
    grid_spec = pltpu.PrefetchScalarGridSpec(
        num_scalar_prefetch=1,
        grid=(n, n_pages // cp),
        in_specs=[spec_q, spec_q, pl.BlockSpec(memory_space=pl.ANY), pl.BlockSpec(memory_space=pl.ANY)],
        out_specs=pl.BlockSpec((None, 8, KV_LORA), lambda b, j, pt: (b, 0, 0)),
        scratch_shapes=[pltpu.VMEM((N_SLOTS, cp, page, KV_LORA), F32),
                        pltpu.VMEM((N_SLOTS, MLA_ROPE, cp * page), F32),
                        pltpu.SemaphoreType.DMA((2, N_SLOTS)),
                        pltpu.VMEM((8, 1), F32), pltpu.VMEM((8, 1), F32), pltpu.VMEM((8, KV_LORA), F32)],
    )
    return pl.pallas_call(
        functools.partial(_mla_dec_kernel, cp=cp, page=page, n_sub=1),
        grid_spec=grid_spec,
        out_shape=jax.ShapeDtypeStruct((n, 8, KV_LORA), F32),
        compiler_params=_cparams("arbitrary", "arbitrary"),
        name="mla_decode",
    )(page_table, q8, kvn8, pool_ckv, pool_kpet)


def _mem_dec_kernel(q_ref, mk_ref, mv_ref, o_ref, *, bt):
    n_col = mk_ref.shape[1]
    col_head = jnp.bitwise_and(lax.broadcasted_iota(jnp.int32, (8, n_col), 1), N_HEADS - 1)
    own = col_head == lax.broadcasted_iota(jnp.int32, (8, n_col), 0)
    pad = jnp.zeros((8 - N_HEADS, HEAD_DIM), BF16)
    scores = [_dot_nt(jnp.concatenate([q_ref[b], pad], axis=0), mk_ref[b].astype(BF16)) for b in range(bt)]
    probs = []
    for b in range(bt):
        s = jnp.where(own, scores[b] * MEM_SCALE, NEG_BIG)
        p = jnp.exp(s - jnp.max(s, axis=-1, keepdims=True))
        probs.append((p / jnp.sum(p, axis=-1, keepdims=True)).astype(BF16))
    for b in range(bt):
        o_ref[b] = _dot(probs[b], mv_ref[b].astype(BF16))[0:N_HEADS, :]


def _mem_decode(mq, mk, mv, bt):
    n, n_col, _ = mk.shape
    row = pl.BlockSpec((bt, N_HEADS, HEAD_DIM), lambda i: (i, 0, 0))
    mem = pl.BlockSpec((bt, n_col, HEAD_DIM), lambda i: (i, 0, 0))
    return pl.pallas_call(
        functools.partial(_mem_dec_kernel, bt=bt),
        grid=(n // bt,),
        in_specs=[row, mem, mem],
        out_specs=row,
        out_shape=jax.ShapeDtypeStruct((n, N_HEADS, HEAD_DIM), F32),
        compiler_params=_cparams("parallel"),
        name="mem_decode",
    )(mq, mk, mv)


def _tiles(seq, n_mem_rows, n_pages):
    return dict(
        tm=min(512, seq),
        tq=min(512, seq),
        t_merge=min(1024, seq),
        t_ret=min(2048, seq),
        t_mem=min(512, n_mem_rows),
        cp=min(64, n_pages),
        bt=8,
    )


def _rope_tables(pos, n_freq, slot):
    inv = ROPE_BASE ** (-jnp.arange(0, 2 * n_freq, 2, dtype=F32) / (2 * n_freq))
    ang = pos.astype(F32)[:, None] * inv[None, :]
    c, s = jnp.cos(ang), jnp.sin(ang)
    z = jnp.zeros((pos.shape[0], slot // 2 - n_freq), F32)
    return jnp.concatenate([c, z, c, z], axis=1), jnp.concatenate([-s, z, s, z], axis=1)


def _spread(w):
    z = jnp.zeros(w.shape[:-1] + (32,), w.dtype)
    return jnp.concatenate([w[..., 0:32], z, w[..., 32:64], z], axis=-1)


def _prep_weights(w_in, w_uq, w_uk, w_uv, w_mem_kv, w_out):
    o = [0, 512, 1024, 1536, 1920, 2176, 2240, 2752, 4288]
    seg = lambda i: w_in[:, o[i]:o[i + 1]]
    win = jnp.concatenate([seg(0), seg(1), seg(2), seg(3), _spread(seg(5)), seg(4), seg(6), seg(7)], axis=1)
    wuq = jnp.concatenate([w_uq[:, :, :MLA_NOPE].reshape(Q_LORA, -1),
                           _spread(w_uq[:, :, MLA_NOPE:]).reshape(Q_LORA, -1)], axis=1)
    wuk = jnp.transpose(w_uk, (1, 2, 0))
    wuv = jnp.transpose(w_uv, (1, 0, 2))
    return tuple(a.astype(BF16) for a in (win, wuq, wuk, wuv, w_mem_kv, w_out))


def kernel(x_prompt, x_sample, mem_prompt, cache_ckv, cache_kpe, page_table, state_ret, cache_mem_k, cache_mem_v,
           norm_g, w_in, ret_gn_g, mla_qnorm_g, w_uq, w_uk, mla_kvnorm_g, w_uv, mem_norm_g, w_mem_kv, w_out,
           final_norm_g):
    batch, seq, d_model = x_prompt.shape
    n_dec = x_sample.shape[0]
    n_mem = mem_prompt.shape[1]
    depth = w_in.shape[0]
    assert depth == 1 and x_sample.shape[1] == 1
    l = 0
    win, wuq, wuk, wuv, wmem, wout = _prep_weights(w_in[l], w_uq[l], w_uk[l], w_uv[l], w_mem_kv[l], w_out[l])
    g_in = norm_g[l][None, :]
    g_q = mla_qnorm_g[l][None, :]
    g_kv = mla_kvnorm_g[l][None, :]
    g_gn = ret_gn_g[l][None, :]
    g_mem = mem_norm_g[l][None, :]
    g_fin = final_norm_g[None, :]

    t = _tiles(seq, batch * n_mem, page_table.shape[1])
    tm, tq = t["tm"], t["tq"]

    xp = x_prompt.reshape(batch * seq, d_model)
    pos_p = jnp.arange(seq, dtype=jnp.int32)
    tabs_p = _rope_tables(pos_p, HEAD_DIM // 2, LANES) + _rope_tables(pos_p, MLA_ROPE // 2, LANES)
    rq, rk, rv, qa, kvb, ckv, mq, sg, kpet, vt = _project(
        xp, tabs_p, g_in, win, g_q, wuq, wuk, g_kv, batch, seq, tm, tq)
    ret_n, ret_state_p = _retention_prompt(rq, rk, rv, g_gn, batch, seq, t["t_ret"])
    mla = _mla_prompt(qa, kvb, vt, wuv, batch, seq, tq)
    mk, mv, mk4, mv4 = _mem_kv(mem_prompt.reshape(batch * n_mem, d_model), g_mem, wmem, t["t_mem"])
    y_p = _merge_prompt(xp, ret_n, mla, mq, sg, mk, mv, wout, g_fin, batch, seq, t["t_merge"])

    xs = x_sample.reshape(n_dec, d_model)
    pos_s = jnp.full((n_dec,), PAST_LEN, dtype=jnp.int32)
    tabs_s = _rope_tables(pos_s, HEAD_DIM // 2, LANES) + _rope_tables(pos_s, MLA_ROPE // 2, LANES)
    rq_s, rk_s, rv_s, qa_s, _, ckv_s, mq_s, sg_s, kpet_s, _ = _project(
        xs, tabs_s, g_in, win, g_q, wuq, wuk, g_kv, 1, n_dec, n_dec, n_dec)
    kpe_s = jnp.swapaxes(kpet_s[0], 0, 1)
    ret_n_s, ret_state_s = _retention_step(rq_s, rk_s, rv_s, state_ret[l], g_gn, t["bt"])
    qh = qa_s.reshape(n_dec, N_HEADS, QK_W)
    q_std = jnp.concatenate([qh[..., 0:KV_LORA], qh[..., KV_LORA:KV_LORA + 32], qh[..., KV_LORA + 64:KV_LORA + 96]], -1)
    q8 = jnp.pad(q_std, ((0, 0), (0, 8 - N_HEADS), (0, 0)))
    kvn8 = jnp.pad(jnp.concatenate([ckv_s, kpe_s], axis=-1)[:, None, :], ((0, 0), (0, 7), (0, 0)))
    lat_s = _mla_decode(q8, kvn8, cache_ckv[l], jnp.swapaxes(cache_kpe[l], 1, 2), page_table, t["cp"])
    lat_s = lat_s[:, 0:N_HEADS, :].reshape(n_dec, N_HEADS * KV_LORA)
    memo_s = _mem_decode(mq_s.reshape(n_dec, N_HEADS, HEAD_DIM),
                         cache_mem_k[l].reshape(n_dec, n_mem * N_HEADS, HEAD_DIM),
                         cache_mem_v[l].reshape(n_dec, n_mem * N_HEADS, HEAD_DIM), t["bt"]).reshape(n_dec, MEM_W)
    y_s = _merge_sample(xs, ret_n_s, lat_s, memo_s, sg_s, wuv, wout, g_fin)

    return (y_p.reshape(batch, seq, d_model), y_s.reshape(n_dec, 1, d_model),
            ckv.reshape(1, batch, seq, KV_LORA), jnp.swapaxes(kpet, 1, 2)[None],
            ret_state_p[None], mk4.reshape(1, batch, n_mem, N_HEADS, HEAD_DIM), mv4.reshape(1, batch, n_mem, N_HEADS, HEAD_DIM),
            ckv_s.reshape(1, n_dec, 1, KV_LORA), kpe_s.reshape(1, n_dec, 1, MLA_ROPE), ret_state_s[None])
```
